```python
import math
import jax
import jax.numpy as jnp
from jax import lax
import numpy as np

D_MODEL = 2048
BATCH = 4
SEQ = 2048
DEPTH = 4

PLE_DIM = 256
RMS_EPS = 1e-6
Q_BLOCK = 128
SCAN_CHUNK = 64
ROPE_THETA = 10000.0

DIFF_HEADS = 4
DIFF_QK = 64
DIFF_V = 128
DIFF_WIDTH = DIFF_HEADS * DIFF_V

HGRN_HEADS = 4
HGRN_DK = 128
HGRN_DV = 128
HGRN_WIDTH = HGRN_HEADS * HGRN_DV

MLA_HEADS = 4
MLA_NOPE = 128
MLA_ROPE = 64
MLA_V = 128
MLA_Q_RANK = 384
MLA_KV_RANK = 256
MLA_WIDTH = MLA_HEADS * MLA_V

MLSTM_HEADS = 4
MLSTM_DQK = 64
MLSTM_DV = 128
MLSTM_CONV = 4
MLSTM_WIDTH = MLSTM_HEADS * MLSTM_DV

D_MIX = DIFF_WIDTH + HGRN_WIDTH + MLA_WIDTH + MLSTM_WIDTH

IN_SIZES = (
    DIFF_HEADS * 2 * DIFF_QK, DIFF_HEADS * 2 * DIFF_QK, DIFF_HEADS * DIFF_V,
    HGRN_HEADS * HGRN_DK, HGRN_HEADS * HGRN_DK, HGRN_HEADS * HGRN_DV, HGRN_HEADS * HGRN_DV,
    MLA_Q_RANK, MLA_KV_RANK, MLA_ROPE,
    MLSTM_HEADS * MLSTM_DQK, MLSTM_HEADS * MLSTM_DQK, MLSTM_HEADS * MLSTM_DV,
    MLSTM_HEADS, MLSTM_HEADS, MLSTM_HEADS * MLSTM_DV,
)
IN_COLS = sum(IN_SIZES)

PEER_HEADS = 8
PEER_N_KEYS = 128
PEER_N_EXPERTS = PEER_N_KEYS * PEER_N_KEYS
PEER_KEY_DIM = 128
PEER_TOPK = 16
PEER_TOKEN_BLOCK = 64

kernel_name = 'hybrid_diffattn_hgrn2_mla_mlstm_peer'


def rmsnorm(x, g):
    xf = x.astype(jnp.float32)
    y = xf * lax.rsqrt(jnp.mean(xf * xf, axis=-1, keepdims=True) + RMS_EPS)
    return (y * g.astype(jnp.float32)).astype(x.dtype)


def head_rmsnorm(y, g):
    return rmsnorm(y, g.reshape(y.shape[-2], y.shape[-1]))


def rope(x, positions):
    half = x.shape[-1] // 2
    inv = ROPE_THETA ** (-jnp.arange(half, dtype=jnp.float32) / half)
    ang = positions.astype(jnp.float32)[:, :, None, None] * inv
    cos, sin = jnp.cos(ang), jnp.sin(ang)
    xf = x.astype(jnp.float32)
    x1, x2 = xf[..., :half], xf[..., half:]
    return jnp.concatenate([x1 * cos - x2 * sin, x2 * cos + x1 * sin], axis=-1).astype(x.dtype)


def causal_conv(x, w, b):
    c = x.shape[-1]
    y = lax.conv_general_dilated(x, w[:, None, :].astype(x.dtype), window_strides=(1,),
                                 padding=((w.shape[0] - 1, 0),),
                                 dimension_numbers=('NWC', 'WIO', 'NWC'),
                                 feature_group_count=c)
    return y + b.astype(x.dtype)


def causal_attention_blocked(q, k, v, scale):
    B, H, S, dk = q.shape
    dv = v.shape[-1]
    nb = S // Q_BLOCK
    qb = q.reshape(B, H, nb, Q_BLOCK, dk).transpose(2, 0, 1, 3, 4)
    k_pos = jnp.arange(S)

    def block(args):
        q_i, i = args
        s = jnp.einsum('bhqd,bhkd->bhqk', q_i, k).astype(jnp.float32) * scale
        q_pos = i * Q_BLOCK + jnp.arange(Q_BLOCK)
        s = jnp.where(k_pos[None, :] <= q_pos[:, None], s, -jnp.inf)
        pr = jax.nn.softmax(s, axis=-1)
        return jnp.einsum('bhqk,bhkd->bhqd', pr.astype(v.dtype), v)

    o = lax.map(block, (qb, jnp.arange(nb)))
    return o.transpose(1, 2, 0, 3, 4).reshape(B, H, S, dv)


def diff_attention_blocked(q, k, v, lam, scale):
    B, H, _, S, d = q.shape
    dv = v.shape[-1]
    nb = S // Q_BLOCK
    qb = q.reshape(B, H, 2, nb, Q_BLOCK, d).transpose(3, 0, 1, 2, 4, 5)
    k_pos = jnp.arange(S)

    def block(args):
        q_i, i = args
        s = jnp.einsum('bhmqd,bhmkd->bhmqk', q_i, k).astype(jnp.float32) * scale
        q_pos = i * Q_BLOCK + jnp.arange(Q_BLOCK)
        s = jnp.where(k_pos[None, :] <= q_pos[:, None], s, -jnp.inf)
        pr = jax.nn.softmax(s, axis=-1)
        a = pr[:, :, 0] - lam * pr[:, :, 1]
        return jnp.einsum('bhqk,bhkd->bhqd', a.astype(v.dtype), v)

    o = lax.map(block, (qb, jnp.arange(nb)))
    return o.transpose(1, 2, 0, 3, 4).reshape(B, H, S, dv)


def gla_chunked(q, k, v, log_f):
    B, H, S, dk = q.shape
    dv = v.shape[-1]
    nc = S // SCAN_CHUNK
    chunks = lambda t: t.reshape(B, H, nc, SCAN_CHUNK, t.shape[-1]).transpose(2, 0, 1, 3, 4)
    causal = jnp.tril(jnp.ones((SCAN_CHUNK, SCAN_CHUNK), dtype=bool))[None, None, :, :, None]

    def step(state, inp):
        qc, kc, vc, gc = inp
        b = jnp.cumsum(gc, axis=2)
        inter = jnp.einsum('bhtd,bhde->bhte', qc * jnp.exp(b), state)
        rel = b[:, :, :, None, :] - b[:, :, None, :, :]
        decay = jnp.exp(jnp.where(causal, rel, -jnp.inf))
        scores = jnp.einsum('bhtd,bhsd,bhtsd->bhts', qc, kc, decay)
        intra = jnp.einsum('bhts,bhse->bhte', scores, vc)
        b_last = b[:, :, -1:, :]
        new_state = (jnp.exp(b_last[:, :, 0, :])[..., None] * state
                     + jnp.einsum('bhsd,bhse->bhde', kc * jnp.exp(b_last - b), vc))
        return new_state, inter + intra

    state0 = jnp.zeros((B, H, dk, dv), jnp.float32)
    _, o = lax.scan(step, state0, (chunks(q), chunks(k), chunks(v), chunks(log_f)))
    return o.transpose(1, 2, 0, 3, 4).reshape(B, H, S, dv)


def mlstm_chunked(q, k, v, log_i, log_f):
    B, H, S, dqk = q.shape
    dv = v.shape[-1]
    nc = S // SCAN_CHUNK
    chunks4 = lambda t: t.reshape(B, H, nc, SCAN_CHUNK, t.shape[-1]).transpose(2, 0, 1, 3, 4)
    chunks3 = lambda t: t.reshape(B, H, nc, SCAN_CHUNK).transpose(2, 0, 1, 3)
    causal = jnp.tril(jnp.ones((SCAN_CHUNK, SCAN_CHUNK), dtype=bool))[None, None]

    def step(carry, inp):
        c_st, n_st, m_st = carry
        qc, kc, vc, lic, lfc = inp
        b = jnp.cumsum(lfc, axis=-1)
        dmat = jnp.where(causal, b[..., :, None] - b[..., None, :] + lic[..., None, :], -jnp.inf)
        inter_log = b + m_st[..., None]
        m_t = jnp.maximum(inter_log, jnp.max(dmat, axis=-1))
        w_intra = jnp.exp(dmat - m_t[..., None])
        w_inter = jnp.exp(inter_log - m_t)
        s = jnp.einsum('bhtd,bhsd->bhts', qc, kc) * w_intra
        num = (w_inter[..., None] * jnp.einsum('bhtd,bhde->bhte', qc, c_st)
               + jnp.einsum('bhts,bhse->bhte', s, vc))
        den = w_inter * jnp.einsum('bhtd,bhd->bht', qc, n_st) + jnp.sum(s, axis=-1)
        h = num / jnp.maximum(jnp.abs(den), jnp.exp(-m_t))[..., None]
        b_last = b[..., -1]
        log_s = b_last[..., None] - b + lic
        m_new = jnp.maximum(b_last + m_st, jnp.max(log_s, axis=-1))
        carry_scale = jnp.exp(b_last + m_st - m_new)
        ws = jnp.exp(log_s - m_new[..., None])
        c_new = carry_scale[..., None, None] * c_st + jnp.einsum('bhs,bhsd,bhse->bhde', ws, kc, vc)
        n_new = carry_scale[..., None] * n_st + jnp.einsum('bhs,bhsd->bhd', ws, kc)
        return (c_new, n_new, m_new), h

    carry0 = (jnp.zeros((B, H, dqk, dv), jnp.float32),
              jnp.zeros((B, H, dqk), jnp.float32),
              jnp.zeros((B, H), jnp.float32))
    _, o = lax.scan(step, carry0, (chunks4(q), chunks4(k), chunks4(v), chunks3(log_i), chunks3(log_f)))
    return o.transpose(1, 2, 0, 3, 4).reshape(B, H, S, dv)


def diff_mixer(z_q, z_k, z_v, lam_vecs, norm_g, lambda_init):
    B, S, _ = z_q.shape
    qk_heads = lambda a: a.reshape(B, S, DIFF_HEADS, 2, DIFF_QK).transpose(0, 2, 3, 1, 4)
    v = z_v.reshape(B, S, DIFF_HEADS, DIFF_V).transpose(0, 2, 1, 3)
    lv = lam_vecs.astype(jnp.float32)
    lam = jnp.exp(jnp.sum(lv[0] * lv[1])) - jnp.exp(jnp.sum(lv[2] * lv[3])) + lambda_init
    o = diff_attention_blocked(qk_heads(z_q), qk_heads(z_k), v, lam, DIFF_QK ** -0.5)
    o = head_rmsnorm(o.transpose(0, 2, 1, 3), norm_g) * (1.0 - lambda_init)
    return o.reshape(B, S, DIFF_WIDTH)


def hgrn2_mixer(z_q, z_f, z_v, z_g, lb, norm_g):
    B, S, _ = z_q.shape
    heads = lambda a, d: a.reshape(B, S, HGRN_HEADS, d).transpose(0, 2, 1, 3).astype(jnp.float32)
    q = heads(z_q, HGRN_DK) * HGRN_DK ** -0.5
    zf = heads(z_f, HGRN_DK)
    lb_h = lb.reshape(HGRN_HEADS, 1, HGRN_DK)
    k = (1.0 - lb_h) * jax.nn.sigmoid(-zf)
    log_f = jnp.logaddexp(jnp.log(lb_h), jnp.log1p(-lb_h) + jax.nn.log_sigmoid(zf))
    v = heads(z_v, HGRN_DV)
    o = gla_chunked(q, k, v, log_f).transpose(0, 2, 1, 3)
    g = jax.nn.silu(z_g.reshape(B, S, HGRN_HEADS, HGRN_DV).astype(jnp.float32))
    return (head_rmsnorm(o, norm_g) * g).reshape(B, S, HGRN_WIDTH)


def mla_mixer(z_cq, z_ckv, z_kr, q_norm, kv_norm, w_uq, w_ukv, positions):
    B, S, _ = z_cq.shape
    q = (rmsnorm(z_cq, q_norm) @ w_uq).reshape(B, S, MLA_HEADS, MLA_NOPE + MLA_ROPE)
    q = jnp.concatenate([q[..., :MLA_NOPE], rope(q[..., MLA_NOPE:], positions)], axis=-1)
    kv = (rmsnorm(z_ckv, kv_norm) @ w_ukv).reshape(B, S, MLA_HEADS, MLA_NOPE + MLA_V)
    k_rope = jnp.broadcast_to(rope(z_kr[:, :, None, :], positions), (B, S, MLA_HEADS, MLA_ROPE))
    k = jnp.concatenate([kv[..., :MLA_NOPE], k_rope], axis=-1)
    v = kv[..., MLA_NOPE:]
    t = lambda a: a.transpose(0, 2, 1, 3)
    o = causal_attention_blocked(t(q), t(k), t(v), (MLA_NOPE + MLA_ROPE) ** -0.5)
    return t(o).reshape(B, S, MLA_WIDTH)


def mlstm_mixer(z_q, z_k, z_v, z_i, z_f, z_o, conv_w, conv_b, gate_b, norm_g):
    B, S, _ = z_q.shape
    hw = MLSTM_HEADS * MLSTM_DQK
    qk = jax.nn.silu(causal_conv(jnp.concatenate([z_q, z_k], axis=-1), conv_w, conv_b)).astype(jnp.float32)
    heads = lambda a, d: a.reshape(B, S, MLSTM_HEADS, d).transpose(0, 2, 1, 3).astype(jnp.float32)
    q = heads(qk[..., :hw], MLSTM_DQK) * MLSTM_DQK ** -0.5
    k = heads(qk[..., hw:], MLSTM_DQK)
    v = heads(z_v, MLSTM_DV)
    gb = gate_b.astype(jnp.float32)
    log_i = (z_i.astype(jnp.float32) + gb[0]).transpose(0, 2, 1)
    log_f = jax.nn.log_sigmoid(z_f.astype(jnp.float32) + gb[1]).transpose(0, 2, 1)
    h = mlstm_chunked(q, k, v, log_i, log_f).transpose(0, 2, 1, 3)
    o = jax.nn.sigmoid(z_o.reshape(B, S, MLSTM_HEADS, MLSTM_DV).astype(jnp.float32))
    return (head_rmsnorm(h, norm_g) * o).reshape(B, S, MLSTM_WIDTH)


def peer_ffn(h, w_q, subkeys, u_tab, v_tab):
    B, S, D = h.shape
    T = B * S
    xt = h.reshape(T, D)
    q = (xt @ w_q).reshape(T, PEER_HEADS, 2, PEER_KEY_DIM)
    scores = jnp.einsum('thcd,hckd->thck', q, subkeys).astype(jnp.float32)
    s_top, i_top = lax.top_k(scores, PEER_TOPK)
    cand = (s_top[:, :, 0, :, None] + s_top[:, :, 1, None, :]).reshape(T, PEER_HEADS, PEER_TOPK * PEER_TOPK)
    cand_idx = (i_top[:, :, 0, :, None] * PEER_N_KEYS + i_top[:, :, 1, None, :]).reshape(T, PEER_HEADS, PEER_TOPK * PEER_TOPK)
    best, pos = lax.top_k(cand, PEER_TOPK)
    idx = jnp.take_along_axis(cand_idx, pos, axis=-1)
    gates = jax.nn.softmax(best, axis=-1)
    nb = T // PEER_TOKEN_BLOCK

    def block(args):
        xb, ib, gb = args
        act = jax.nn.gelu(jnp.einsum('td,ted->te', xb, u_tab[ib]).astype(jnp.float32), approximate=False)
        return jnp.einsum('te,ted->td', (gb * act).astype(xb.dtype), v_tab[ib])

    y = lax.map(block, (xt.reshape(nb, PEER_TOKEN_BLOCK, D),
                        idx.reshape(nb, PEER_TOKEN_BLOCK, PEER_HEADS * PEER_TOPK),
                        gates.reshape(nb, PEER_TOKEN_BLOCK, PEER_HEADS * PEER_TOPK)))
    return y.reshape(B, S, D)


def setup_inputs(seed: int = 0) -> dict:
    key = jax.random.key(seed)
    ks = jax.random.split(key, 32)
    f32 = jnp.float32
    L = DEPTH

    def nrm(k, shape, scale):
        return scale * jax.random.normal(k, shape, f32)

    def gain(k, shape):
        return 1.0 + 0.1 * jax.random.normal(k, shape, f32)

    positions = (jax.random.randint(ks[2], (BATCH, 1), 0, 4096, dtype=jnp.int32)
                 + jnp.arange(SEQ, dtype=jnp.int32)[None, :])
    mlstm_gate_b = jnp.stack([nrm(ks[15], (L, MLSTM_HEADS), 0.1),
                              3.0 + nrm(ks[16], (L, MLSTM_HEADS), 0.5)], axis=1)
    return {
        'x': nrm(ks[0], (BATCH, SEQ, D_MODEL), 1.0),
        'p': nrm(ks[1], (DEPTH, BATCH, SEQ, PLE_DIM), 1.0),
        'positions': positions,
        'ln_mix': gain(ks[3], (L, D_MODEL)),
        'w_in': nrm(ks[4], (L, D_MODEL, IN_COLS), D_MODEL ** -0.5),
        'diff_lambda': nrm(ks[5], (L, 4, DIFF_QK), 0.1),
        'diff_norm': gain(ks[6], (L, DIFF_WIDTH)),
        'hgrn_lb_logits': nrm(ks[7], (L, HGRN_HEADS * HGRN_DK), 0.1),
        'hgrn_norm': gain(ks[8], (L, HGRN_WIDTH)),
        'mla_q_norm': gain(ks[9], (L, MLA_Q_RANK)),
        'mla_kv_norm': gain(ks[10], (L, MLA_KV_RANK)),
        'mla_w_uq': nrm(ks[11], (L, MLA_Q_RANK, MLA_HEADS * (MLA_NOPE + MLA_ROPE)), MLA_Q_RANK ** -0.5),
        'mla_w_ukv': nrm(ks[12], (L, MLA_KV_RANK, MLA_HEADS * (MLA_NOPE + MLA_V)), MLA_KV_RANK ** -0.5),
        'mlstm_conv_w': nrm(ks[13], (L, MLSTM_CONV, 2 * MLSTM_HEADS * MLSTM_DQK), MLSTM_CONV ** -0.5),
        'mlstm_conv_b': nrm(ks[14], (L, 2 * MLSTM_HEADS * MLSTM_DQK), 0.01),
        'mlstm_gate_b': mlstm_gate_b,
        'mlstm_norm': gain(ks[17], (L, MLSTM_WIDTH)),
        'group_gain': gain(ks[18], (L, D_MIX)),
        'w_out': nrm(ks[19], (L, D_MIX, D_MODEL), D_MIX ** -0.5),
        'ln_ffn': gain(ks[20], (L, D_MODEL)),
        'peer_w_q': nrm(ks[21], (L, D_MODEL, PEER_HEADS * 2 * PEER_KEY_DIM), D_MODEL ** -0.5),
        'peer_subkeys': nrm(ks[22], (L, PEER_HEADS, 2, PEER_N_KEYS, PEER_KEY_DIM), PEER_KEY_DIM ** -0.5),
        'peer_u': nrm(ks[23], (L, PEER_N_EXPERTS, D_MODEL), D_MODEL ** -0.5),
        'peer_v': nrm(ks[24], (L, PEER_N_EXPERTS, D_MODEL), 0.25),
        'ln_ple': gain(ks[25], (L, D_MODEL)),
        'ple_w_gate': nrm(ks[26], (L, D_MODEL, D_MODEL), D_MODEL ** -0.5),
        'ple_w_proj': nrm(ks[27], (L, PLE_DIM, D_MODEL), PLE_DIM ** -0.5),
        'ln_final': gain(ks[28], (D_MODEL,)),
    }


def reference(x, p, positions, ln_mix, w_in, diff_lambda, diff_norm, hgrn_lb_logits, hgrn_norm,
              mla_q_norm, mla_kv_norm, mla_w_uq, mla_w_ukv, mlstm_conv_w, mlstm_conv_b, mlstm_gate_b,
              mlstm_norm, group_gain, w_out, ln_ffn, peer_w_q, peer_subkeys, peer_u, peer_v,
              ln_ple, ple_w_gate, ple_w_proj, ln_final):
    lbs = jnp.cumsum(jax.nn.softmax(hgrn_lb_logits.astype(jnp.float32), axis=0), axis=0)
    lbs = lbs - lbs[:1]
    split_at = np.cumsum(IN_SIZES)[:-1].tolist()
    h = x
    for i in range(DEPTH):
        hn = rmsnorm(h, ln_mix[i])
        z = hn @ w_in[i]
        (d_q, d_k, d_v, g_q, g_f, g_v, g_g, m_cq, m_ckv, m_kr,
         l_q, l_k, l_v, l_i, l_f, l_o) = jnp.split(z, split_at, axis=-1)
        lambda_init = 0.8 - 0.6 * math.exp(-0.3 * i)
        y_diff = diff_mixer(d_q, d_k, d_v, diff_lambda[i], diff_norm[i], lambda_init)
        y_hgrn = hgrn2_mixer(g_q, g_f, g_v, g_g, lbs[i], hgrn_norm[i])
        y_mla = mla_mixer(m_cq, m_ckv, m_kr, mla_q_norm[i], mla_kv_norm[i], mla_w_uq[i], mla_w_ukv[i], positions)
        y_mlstm = mlstm_mixer(l_q, l_k, l_v, l_i, l_f, l_o, mlstm_conv_w[i], mlstm_conv_b[i],
                              mlstm_gate_b[i], mlstm_norm[i])
        y = jnp.concatenate([y_diff, y_hgrn, y_mla, y_mlstm], axis=-1).astype(x.dtype) * group_gain[i]
        h = h + y @ w_out[i]
        h = h + peer_ffn(rmsnorm(h, ln_ffn[i]), peer_w_q[i], peer_subkeys[i], peer_u[i], peer_v[i])
        gate = jax.nn.sigmoid(rmsnorm(h, ln_ple[i]) @ ple_w_gate[i])
        h = h + gate * (p[i] @ ple_w_proj[i])
    return rmsnorm(h, ln_final)
```

```python
import functools
import math

import jax
import jax.numpy as jnp
from jax import lax
from jax.experimental import pallas as pl
from jax.experimental.pallas import tpu as pltpu

F32 = jnp.float32
BF16 = jnp.bfloat16

D_MODEL = 2048
PLE_DIM = 256
RMS_EPS = 1e-6
ROPE_THETA = 10000.0
HEADS = 4
HEAD_V = 128
MIX_W = HEADS * HEAD_V
DIFF_QK = 64
HGRN_DK = 128
MLA_NOPE = 128
MLA_ROPE = 64
MLA_Q_RANK = 384
MLA_KV_RANK = 256
MLA_QK_PAD = 256
MLSTM_DQK = 64
MLSTM_CONV = 4
CHUNK = 64
SUB = 16
PEER_HEADS = 8
PEER_KEYS = 128
PEER_KEY_DIM = 128
PEER_TOPK = 16
PEER_EXPERTS = PEER_KEYS * PEER_KEYS
NEG = -1e30

VMEM_LIMIT = 56 * 1024 * 1024
LANES = 128
SUBLANES = 8

IN_SIZES = (512, 512, 512, 512, 512, 512, 512, 384, 256, 64, 256, 256, 512, 4, 4, 512)
C_DQ, C_DK, C_DV, C_GQ, C_GF, C_GV, C_GG, C_LV, C_LO = (i * 512 for i in range(9))
C_CKV, C_LQ, C_LK = 4608, 4864, 5120
C_CQ = 5376
C_KR, C_KRR, C_GATE = 5760, 5888, 6016
N_IN = 6144


def _cparams(sem):
    return pltpu.CompilerParams(dimension_semantics=sem, vmem_limit_bytes=VMEM_LIMIT)


def _rms(x, g):
    return x * lax.rsqrt(jnp.mean(x * x, axis=-1, keepdims=True) + RMS_EPS) * g


def _dot(a, b):
    return jnp.dot(a, b, preferred_element_type=F32)


def _dot_nt(a, b):
    return lax.dot_general(a, b, (((1,), (1,)), ((), ())), preferred_element_type=F32)


def _dot_tn(a, b):
    return lax.dot_general(a, b, (((0,), (0,)), ((), ())), preferred_element_type=F32)


def _log_sigmoid(x):
    return jnp.minimum(x, 0.0) - jnp.log1p(jnp.exp(-jnp.abs(x)))


def _cumsum_rows(x):
    row = lax.broadcasted_iota(jnp.int32, x.shape, 0)
    k = 1
    while k < x.shape[0]:
        x = x + jnp.where(row >= k, pltpu.roll(x, k, 0), 0.0)
        k *= 2
    return x


def _norm_mm_body(x_ref, g_ref, w_ref, o_ref, *rest, emit_xn):
    xn_ref = rest[-1]

    @pl.when(pl.program_id(1) == 0)
    def _():
        y = _rms(x_ref[...], g_ref[...]).astype(BF16)
        xn_ref[...] = y
        if emit_xn:
            rest[0][...] = y

    o_ref[...] = _dot(xn_ref[...], w_ref[...]).astype(o_ref.dtype)


def _norm_mm(x, g, w, out_dtype, tm, tn, emit_xn=False):
    t, d = x.shape
    n = w.shape[1]
    out_shape = [jax.ShapeDtypeStruct((t, n), out_dtype)]
    out_specs = [pl.BlockSpec((tm, tn), lambda i, j: (i, j))]
    if emit_xn:
        out_shape.append(jax.ShapeDtypeStruct((t, d), BF16))
        out_specs.append(pl.BlockSpec((tm, d), lambda i, j: (i, 0)))
    res = pl.pallas_call(
        functools.partial(_norm_mm_body, emit_xn=emit_xn),
        grid=(t // tm, n // tn),
        in_specs=[pl.BlockSpec((tm, d), lambda i, j: (i, 0)),
                  pl.BlockSpec((1, d), lambda i, j: (0, 0)),
                  pl.BlockSpec((d, tn), lambda i, j: (0, j))],
        out_specs=out_specs,
        out_shape=out_shape,
        scratch_shapes=[pltpu.VMEM((tm, d), BF16)],
        compiler_params=_cparams(("parallel", "arbitrary")),
        name="norm_mm",
    )(x, g, w)
    return res if emit_xn else res[0]


def _causal_mask(qi, tq, seq):
    row = qi * tq + lax.broadcasted_iota(jnp.int32, (tq, seq), 0)
    col = lax.broadcasted_iota(jnp.int32, (tq, seq), 1)
    return col <= row


def _softmax_terms(s, mask):
    s = jnp.where(mask, s, NEG)
    e = jnp.exp(s - jnp.max(s, axis=-1, keepdims=True))
    return e, 1.0 / jnp.sum(e, axis=-1, keepdims=True)


def _diff_attn_body(lam_ref, q_ref, k_ref, v_ref, g_ref, o_ref, *, tq, seq, lam_init):
    mask = _causal_mask(pl.program_id(2), tq, seq)
    q = q_ref[...] * (DIFF_QK ** -0.5)
    lane = lax.broadcasted_iota(jnp.int32, q.shape, 1)
    k = k_ref[...].astype(BF16)
    e1, r1 = _softmax_terms(_dot_nt(jnp.where(lane < DIFF_QK, q, 0.0).astype(BF16), k), mask)
    e2, r2 = _softmax_terms(_dot_nt(jnp.where(lane >= DIFF_QK, q, 0.0).astype(BF16), k), mask)
    a = e1 * r1 - e2 * (lam_ref[0, 0] * r2)
    o = _dot(a.astype(BF16), v_ref[...].astype(BF16))
    o_ref[...] = _rms(o, g_ref[...]) * (1.0 - lam_init)


def _diff_attn(z, lam, g, batch, seq, lam_init, tq=256):
    nq = seq // tq
    return pl.pallas_call(
        functools.partial(_diff_attn_body, tq=tq, seq=seq, lam_init=lam_init),
        grid=(batch, HEADS, nq),
        in_specs=[pl.BlockSpec(memory_space=pltpu.SMEM),
                  pl.BlockSpec((tq, LANES), lambda b, h, i: (b * nq + i, C_DQ // LANES + h)),
                  pl.BlockSpec((seq, LANES), lambda b, h, i: (b, C_DK // LANES + h)),
                  pl.BlockSpec((seq, LANES), lambda b, h, i: (b, C_DV // LANES + h)),
                  pl.BlockSpec((1, LANES), lambda b, h, i: (0, h))],
        out_specs=pl.BlockSpec((tq, LANES), lambda b, h, i: (b * nq + i, h)),
        out_shape=jax.ShapeDtypeStruct((batch * seq, MIX_W), F32),
        compiler_params=_cparams(("parallel", "parallel", "arbitrary")),
        name="diff_attn",
    )(lam, z, z, z, g)


def _mla_proj_body(cq_ref, ckv_ref, kr_ref, krr_ref, pos_ref, inv_ref, qn_ref, kvn_ref,
                   wq_ref, wkv_ref, q_ref, k_ref, v_ref):
    qa = _dot(_rms(cq_ref[...], qn_ref[...]).astype(BF16), wq_ref[...])
    kva = _dot(_rms(ckv_ref[...], kvn_ref[...]).astype(BF16), wkv_ref[...])
    ang = pos_ref[...] * inv_ref[...]
    cos, sin = jnp.cos(ang), jnp.sin(ang)
    k_rope = (kr_ref[...] * cos + krr_ref[...] * sin).astype(BF16)
    scale = (MLA_NOPE + MLA_ROPE) ** -0.5
    for h in range(HEADS):
        lo, hi = h * LANES, (h + 1) * LANES
        q_rope = qa[:, MIX_W + lo:MIX_W + hi] * cos + qa[:, 2 * MIX_W + lo:2 * MIX_W + hi] * sin
        q_ref[:, 2 * lo:2 * lo + LANES] = (qa[:, lo:hi] * scale).astype(BF16)
        q_ref[:, 2 * lo + LANES:2 * hi] = (q_rope * scale).astype(BF16)
        k_ref[:, 2 * lo:2 * lo + LANES] = kva[:, lo:hi].astype(BF16)
        k_ref[:, 2 * lo + LANES:2 * hi] = k_rope
    v_ref[...] = kva[:, MIX_W:].astype(BF16)


def _mla_proj(z, pos, inv, qn, kvn, wq, wkv, tm=512):
    t = z.shape[0]
    row = lambda c, w: pl.BlockSpec((tm, w), lambda i: (i, c // w))
    full = lambda a: pl.BlockSpec(a.shape, lambda i: (0, 0))
    return pl.pallas_call(
        _mla_proj_body,
        grid=(t // tm,),
        in_specs=[row(C_CQ, MLA_Q_RANK), row(C_CKV, MLA_KV_RANK), row(C_KR, LANES), row(C_KRR, LANES),
                  pl.BlockSpec((tm, 1), lambda i: (i, 0)), full(inv), full(qn), full(kvn), full(wq), full(wkv)],
        out_specs=[pl.BlockSpec((tm, HEADS * MLA_QK_PAD), lambda i: (i, 0)),
                   pl.BlockSpec((tm, HEADS * MLA_QK_PAD), lambda i: (i, 0)),
                   pl.BlockSpec((tm, MIX_W), lambda i: (i, 0))],
        out_shape=[jax.ShapeDtypeStruct((t, HEADS * MLA_QK_PAD), BF16),
                   jax.ShapeDtypeStruct((t, HEADS * MLA_QK_PAD), BF16),
                   jax.ShapeDtypeStruct((t, MIX_W), BF16)],
        compiler_params=_cparams(("parallel",)),
        name="mla_proj",
    )(z, z, z, z, pos, inv, qn, kvn, wq, wkv)


def _mla_attn_body(q_ref, k_ref, v_ref, o_ref, *, tq, seq):
    mask = _causal_mask(pl.program_id(2), tq, seq)
    e, r = _softmax_terms(_dot_nt(q_ref[...], k_ref[...]), mask)
    o_ref[...] = _dot((e * r).astype(BF16), v_ref[...])


def _mla_attn(q, k, v, batch, seq, tq=256):
    nq = seq // tq
    return pl.pallas_call(
        functools.partial(_mla_attn_body, tq=tq, seq=seq),
        grid=(batch, HEADS, nq),
        in_specs=[pl.BlockSpec((tq, MLA_QK_PAD), lambda b, h, i: (b * nq + i, h)),
                  pl.BlockSpec((seq, MLA_QK_PAD), lambda b, h, i: (b, h)),
                  pl.BlockSpec((seq, LANES), lambda b, h, i: (b, h))],
        out_specs=pl.BlockSpec((tq, LANES), lambda b, h, i: (b * nq + i, h)),
        out_shape=jax.ShapeDtypeStruct((batch * seq, MIX_W), F32),
        compiler_params=_cparams(("parallel", "parallel", "arbitrary")),
        name="mla_attn",
    )(q, k, v)


def _hgrn_body(q_ref, f_ref, v_ref, g_ref, par_ref, o_ref, st_ref):
    @pl.when(pl.program_id(1) == 0)
    def _():
        st_ref[...] = jnp.zeros_like(st_ref)

    sub_row = lax.broadcasted_iota(jnp.int32, (SUB, SUB), 0)
    sub_col = lax.broadcasted_iota(jnp.int32, (SUB, SUB), 1)
    for h in range(HEADS):
        sl = slice(h * LANES, (h + 1) * LANES)
        log_lb, log_1m_lb, one_m_lb, gain = (par_ref[r:r + 1, sl] for r in range(4))
        zf = f_ref[:, sl]
        q = q_ref[:, sl] * (HGRN_DK ** -0.5)
        v = v_ref[:, sl].astype(BF16)
        k = one_m_lb * jax.nn.sigmoid(-zf)
        lf_a, lf_b = log_lb, log_1m_lb + _log_sigmoid(zf)
        log_f = jnp.maximum(lf_a, lf_b) + jnp.log1p(jnp.exp(-jnp.abs(lf_a - lf_b)))
        b = _cumsum_rows(log_f)
        st = st_ref[h]
        o = _dot_nt((q * jnp.exp(b)).astype(BF16), st.astype(BF16))
        parts = []
        for i in range(CHUNK // SUB):
            lo = i * SUB
            qi, bi, ki, vi = q[lo:lo + SUB], b[lo:lo + SUB], k[lo:lo + SUB], v[lo:lo + SUB]
            sc = jnp.zeros((SUB, SUB), F32)
            for s in range(SUB):
                dec = jnp.exp(jnp.minimum(bi - bi[s:s + 1], 0.0))
                col = jnp.sum(qi * ki[s:s + 1] * dec, axis=-1, keepdims=True)
                sc = sc + jnp.where(sub_col == s, col, 0.0)
            sc = jnp.where(sub_col <= sub_row, sc, 0.0)
            part = _dot(sc.astype(BF16), vi)
            if i > 0:
                b0 = b[lo - 1:lo]
                qe = (qi * jnp.exp(bi - b0)).astype(BF16)
                ke = (k[:lo] * jnp.exp(b0 - b[:lo])).astype(BF16)
                part = part + _dot(_dot_nt(qe, ke).astype(BF16), v[:lo])
            parts.append(part)
        o = o + jnp.concatenate(parts, axis=0)
        b_last = b[CHUNK - 1:CHUNK]
        st_ref[h] = st * jnp.exp(b_last) + _dot_tn(v, (k * jnp.exp(b_last - b)).astype(BF16))
        o_ref[:, sl] = _rms(o, gain) * jax.nn.silu(g_ref[:, sl])


def _hgrn(z, par, batch, seq):
    nc = seq // CHUNK
    blk = lambda c: pl.BlockSpec((CHUNK, MIX_W), lambda b, i: (b * nc + i, c // MIX_W))
    return pl.pallas_call(
        _hgrn_body,
        grid=(batch, nc),
        in_specs=[blk(C_GQ), blk(C_GF), blk(C_GV), blk(C_GG), pl.BlockSpec((8, MIX_W), lambda b, i: (0, 0))],
        out_specs=pl.BlockSpec((CHUNK, MIX_W), lambda b, i: (b * nc + i, 0)),
        out_shape=jax.ShapeDtypeStruct((batch * seq, MIX_W), F32),
        scratch_shapes=[pltpu.VMEM((HEADS, HEAD_V, HGRN_DK), F32)],
        compiler_params=_cparams(("parallel", "arbitrary")),
        name="hgrn2",
    )(z, z, z, z, par)


def _conv_silu_body(x_ref, w_ref, b_ref, o_ref):
    x = x_ref[...]
    row = lax.broadcasted_iota(jnp.int32, x.shape, 0)
    y = x * w_ref[MLSTM_CONV - 1:MLSTM_CONV] + b_ref[...]
    for k in range(1, MLSTM_CONV):
        y = y + jnp.where(row >= k, pltpu.roll(x, k, 0), 0.0) * w_ref[MLSTM_CONV - 1 - k:MLSTM_CONV - k]
    o_ref[...] = jax.nn.silu(y)


def _conv_silu(z, w, b, batch, seq):
    wd = HEADS * MLSTM_DQK
    return pl.pallas_call(
        _conv_silu_body,
        grid=(batch, 2),
        in_specs=[pl.BlockSpec((seq, wd), lambda bi, j: (bi, C_LQ // wd + j)),
                  pl.BlockSpec((MLSTM_CONV, wd), lambda bi, j: (0, j)),
                  pl.BlockSpec((1, wd), lambda bi, j: (0, j))],
        out_specs=pl.BlockSpec((seq, wd), lambda bi, j: (bi, j)),
        out_shape=jax.ShapeDtypeStruct((batch * seq, 2 * wd), F32),
        compiler_params=_cparams(("parallel", "parallel")),
        name="conv_silu",
    )(z, w, b)


def _mlstm_body(qk_ref, v_ref, gate_ref, og_ref, gb_ref, norm_ref, o_ref, c_ref, m_ref):
    @pl.when(pl.program_id(1) == 0)
    def _():
        c_ref[...] = jnp.zeros_like(c_ref)
        m_ref[...] = jnp.zeros_like(m_ref)

    g = gate_ref[...] + gb_ref[...]
    lane = lax.broadcasted_iota(jnp.int32, g.shape, 1)
    gt = jnp.where(lane < HEADS, g, _cumsum_rows(_log_sigmoid(g)))
    gt_t = gt.T
    row = lax.broadcasted_iota(jnp.int32, (CHUNK, CHUNK), 0)
    col = lax.broadcasted_iota(jnp.int32, (CHUNK, CHUNK), 1)
    ones_col = jnp.where(lane == 0, 1.0, 0.0).astype(BF16)
    half_row = lax.broadcasted_iota(jnp.int32, (LANES, 1), 0) < MLSTM_DQK
    for p in range(HEADS // 2):
        q_t = qk_ref[:, p * LANES:(p + 1) * LANES] * (MLSTM_DQK ** -0.5)
        k_t = qk_ref[:, HEADS * MLSTM_DQK + p * LANES:HEADS * MLSTM_DQK + (p + 1) * LANES]
        c_old = c_ref[p]
        c_bf = c_old.astype(BF16)
        upd = jnp.zeros_like(c_old)
        carries = []
        for u in range(2):
            h = 2 * p + u
            mine = (lane >= u * MLSTM_DQK) & (lane < (u + 1) * MLSTM_DQK)
            li_col, b_col = gt[:, h:h + 1], gt[:, HEADS + h:HEADS + h + 1]
            li_row, b_row = gt_t[h:h + 1, :], gt_t[HEADS + h:HEADS + h + 1, :]
            m_st = m_ref[h:h + 1, 0:1]
            dmat = jnp.where(col <= row, b_col - b_row + li_row, NEG)
            inter_log = b_col + m_st
            m_t = jnp.maximum(inter_log, jnp.max(dmat, axis=-1, keepdims=True))
            qm = jnp.where(mine, q_t, 0.0).astype(BF16)
            s = _dot_nt(qm, k_t.astype(BF16)) * jnp.exp(dmat - m_t)
            v_aug = jnp.concatenate([v_ref[:, h * LANES:(h + 1) * LANES].astype(BF16), ones_col], axis=1)
            out = jnp.exp(inter_log - m_t) * _dot(qm, c_bf) + _dot(s.astype(BF16), v_aug)
            hid = out[:, :HEAD_V] / jnp.maximum(jnp.abs(out[:, HEAD_V:HEAD_V + 1]), jnp.exp(-m_t))
            sl = slice(h * LANES, (h + 1) * LANES)
            o_ref[:, sl] = _rms(hid, norm_ref[:, sl]) * jax.nn.sigmoid(og_ref[:, sl])
            b_last = b_col[CHUNK - 1:CHUNK]
            log_s = b_last - b_col + li_col
            m_new = jnp.maximum(b_last + m_st, jnp.max(log_s, axis=0, keepdims=True))
            carries.append(jnp.exp(b_last + m_st - m_new))
            kw = (jnp.where(mine, k_t, 0.0) * jnp.exp(log_s - m_new)).astype(BF16)
            upd = upd + _dot_tn(kw, v_aug)
            m_ref[h:h + 1, :] = jnp.broadcast_to(m_new, (1, LANES))
        c_ref[p] = jnp.where(half_row, carries[0], carries[1]) * c_old + upd


def _mlstm(qk, z, gb, norm, batch, seq):
    nc = seq // CHUNK
    return pl.pallas_call(
        _mlstm_body,
        grid=(batch, nc),
        in_specs=[pl.BlockSpec((CHUNK, 2 * HEADS * MLSTM_DQK), lambda b, i: (b * nc + i, 0)),
                  pl.BlockSpec((CHUNK, MIX_W), lambda b, i: (b * nc + i, C_LV // MIX_W)),
                  pl.BlockSpec((CHUNK, LANES), lambda b, i: (b * nc + i, C_GATE // LANES)),
                  pl.BlockSpec((CHUNK, MIX_W), lambda b, i: (b * nc + i, C_LO // MIX_W)),
                  pl.BlockSpec((1, LANES), lambda b, i: (0, 0)),
                  pl.BlockSpec((1, MIX_W), lambda b, i: (0, 0))],
        out_specs=pl.BlockSpec((CHUNK, MIX_W), lambda b, i: (b * nc + i, 0)),
        out_shape=jax.ShapeDtypeStruct((batch * seq, MIX_W), F32),
        scratch_shapes=[pltpu.VMEM((HEADS // 2, LANES, 2 * HEAD_V), F32), pltpu.VMEM((8, LANES), F32)],
        compiler_params=_cparams(("parallel", "arbitrary")),
        name="mlstm",
    )(qk, z, z, z, gb, norm)


def _out_proj_body(h_ref, y0_ref, y1_ref, y2_ref, y3_ref, g_ref, w_ref, o_ref):
    acc = h_ref[...]
    for i, y_ref in enumerate((y0_ref, y1_ref, y2_ref, y3_ref)):
        sl = slice(i * MIX_W, (i + 1) * MIX_W)
        acc = acc + _dot((y_ref[...] * g_ref[:, sl]).astype(BF16), w_ref[sl, :])
    o_ref[...] = acc


def _out_proj(h, ys, g, w, tm=512):
    t, d = h.shape
    yspec = pl.BlockSpec((tm, MIX_W), lambda i: (i, 0))
    return pl.pallas_call(
        _out_proj_body,
        grid=(t // tm,),
        in_specs=[pl.BlockSpec((tm, d), lambda i: (i, 0)), yspec, yspec, yspec, yspec,
                  pl.BlockSpec((1, HEADS * MIX_W), lambda i: (0, 0)),
                  pl.BlockSpec((HEADS * MIX_W, d), lambda i: (0, 0))],
        out_specs=pl.BlockSpec((tm, d), lambda i: (i, 0)),
        out_shape=jax.ShapeDtypeStruct((t, d), F32),
        compiler_params=_cparams(("parallel",)),
        name="out_proj",
    )(h, *ys, g, w)


def _ranked_top(x, n, dst_ref):
    rank = jnp.full(x.shape, float(n), F32)
    for r in range(n):
        mx = jnp.max(x, axis=0, keepdims=True)
        dst_ref[r:r + 1, :] = mx
        eq = x == mx
        rank = jnp.where(eq, float(r), rank)
        x = jnp.where(eq, NEG, x)
    return rank


def _kth_and_next(x, k):
    cum = jnp.zeros((1, x.shape[1]), F32)
    kth = jnp.full((1, x.shape[1]), NEG, F32)
    nxt = kth
    for _ in range(k + 1):
        mx = jnp.max(x, axis=0, keepdims=True)
        eq = x == mx
        new = cum + jnp.sum(eq.astype(F32), axis=0, keepdims=True)
        kth = jnp.where((cum < k) & (new >= k), mx, kth)
        nxt = jnp.where((cum < k + 1) & (new >= k + 1), mx, nxt)
        cum = new
        x = jnp.where(eq, NEG, x)
    return kth, nxt


PAIR_COLS = tuple((PEER_TOPK + 1) // (i + 1) for i in range(PEER_TOPK + 1))
PAIR_OFFS = tuple(sum(PAIR_COLS[:i]) for i in range(PEER_TOPK + 2))
PAIR_ROWS = -(-PAIR_OFFS[-1] // SUBLANES) * SUBLANES


def _pack_halves(x):
    bits = pltpu.bitcast(x.astype(BF16).astype(F32), jnp.uint32)
    half = x.shape[0] // 2
    return (bits[half:] & jnp.uint32(0xFFFF0000)) | (bits[:half] >> 16)


def _peer_router_body(q_ref, sk_ref, nb_ref, e1_ref, rb_ref, e2_ref, a_ref, b_ref, g_ref):
    tt = q_ref.shape[0]
    for h in range(PEER_HEADS):
        s1 = _dot_nt(sk_ref[2 * h], q_ref[:, 2 * h * LANES:(2 * h + 1) * LANES])
        s2 = _dot_nt(sk_ref[2 * h + 1], q_ref[:, (2 * h + 1) * LANES:(2 * h + 2) * LANES])
        rank_a = _ranked_top(s1, PEER_TOPK + 1, a_ref)
        rank_b = _ranked_top(s2, PEER_TOPK + 1, b_ref)
        g_ref[PAIR_ROWS - SUBLANES:PAIR_ROWS, :] = jnp.full((SUBLANES, tt), NEG, F32)
        for i, n in enumerate(PAIR_COLS):
            g_ref[PAIR_OFFS[i]:PAIR_OFFS[i] + n, :] = a_ref[i:i + 1, :] + b_ref[0:n, :]
        pair_sums = g_ref[...]
        kth, nxt = _kth_and_next(pair_sums, PEER_TOPK)
        thr = 0.5 * (kth + nxt)
        a_top, b_top = a_ref[0:1, :], b_ref[0:1, :]
        chosen = pair_sums >= thr
        zsum = jnp.sum(jnp.where(chosen, jnp.exp(pair_sums - (a_top + b_top)), 0.0), axis=0, keepdims=True)
        ones = jnp.where(chosen, 1.0, 0.0)
        count_a = jnp.zeros_like(s1)
        for i in range(PEER_TOPK):
            n_i = jnp.sum(ones[PAIR_OFFS[i]:PAIR_OFFS[i + 1]], axis=0, keepdims=True)
            count_a = jnp.where(rank_a == float(i), n_i, count_a)
        e1 = jnp.exp(s1 - a_top).astype(BF16).astype(F32)
        for q in range(tt // LANES):
            nb_ref[q, h] = count_a[:, q * LANES:(q + 1) * LANES]
            e1_ref[q, h] = e1[:, q * LANES:(q + 1) * LANES]
        rb_ref[h] = _pack_halves(rank_b)
        e2_ref[h] = _pack_halves(jnp.exp(s2 - b_top) * (0.5 / zsum))


def _peer_router(q, sk, tt=256):
    t = q.shape[0]
    row_spec = pl.BlockSpec((tt // LANES, PEER_HEADS, PEER_KEYS, LANES), lambda i: (i, 0, 0, 0))
    row_shape = jax.ShapeDtypeStruct((t // LANES, PEER_HEADS, PEER_KEYS, LANES), F32)
    key_spec = pl.BlockSpec((PEER_HEADS, PEER_KEYS // 2, tt), lambda i: (0, 0, i))
    key_shape = jax.ShapeDtypeStruct((PEER_HEADS, PEER_KEYS // 2, t), jnp.uint32)
    return pl.pallas_call(
        _peer_router_body,
        grid=(t // tt,),
        in_specs=[pl.BlockSpec((tt, q.shape[1]), lambda i: (i, 0)),
                  pl.BlockSpec(sk.shape, lambda i: (0, 0, 0))],
        out_specs=[row_spec, row_spec, key_spec, key_spec],
        out_shape=[row_shape, row_shape, key_shape, key_shape],
        scratch_shapes=[pltpu.VMEM((24, tt), F32), pltpu.VMEM((24, tt), F32),
                        pltpu.VMEM((PAIR_ROWS, tt), F32)],
        compiler_params=_cparams(("parallel",)),
        name="peer_router",
    )(q, sk)


PEER_MROWS = 256


def _peer_dense_body(xt_ref, u_ref, vt_ref, nb_ref, e1_ref, rb_ref, e2_ref, o_ref, act_ref, wa_ref, *, rows):
    @pl.when(pl.program_id(1) == 0)
    def _():
        o_ref[...] = jnp.zeros_like(o_ref)

    et, tt = act_ref.shape
    half = PEER_KEYS // 2
    for m in range(et // PEER_MROWS):
        ms = slice(m * PEER_MROWS, (m + 1) * PEER_MROWS)
        act_ref[ms, :] = _dot(u_ref[ms, :], xt_ref[...])

    def gelu2(z):
        return z + z * lax.erf(z * (2.0 ** -0.5))

    for a in range(rows):
        for q in range(tt // LANES):
            tok = slice(q * LANES, (q + 1) * LANES)

            def row(ref, h):
                bits = pltpu.bitcast(jnp.broadcast_to(ref[q, h, a:a + 1, :], (SUBLANES, LANES)), jnp.uint32)
                return pltpu.bitcast((bits & jnp.uint32(0xFFFF0000)) | (bits >> 16), BF16)

            nbs = [row(nb_ref, h) for h in range(PEER_HEADS)]
            e1s = [row(e1_ref, h) for h in range(PEER_HEADS)]
            for s in range(half // SUBLANES):
                ks = slice(s * SUBLANES, (s + 1) * SUBLANES)
                w = None
                for h in range(PEER_HEADS):
                    rb = pltpu.bitcast(rb_ref[h, ks, tok], BF16)
                    e2 = pltpu.bitcast(e2_ref[h, ks, tok], BF16)
                    t = jnp.where(rb < nbs[h], e2 * e1s[h], 0.0)
                    w = t if w is None else w + t
                wbits = pltpu.bitcast(w, jnp.uint32)
                lo = slice(a * PEER_KEYS + s * SUBLANES, a * PEER_KEYS + (s + 1) * SUBLANES)
                hi = slice(a * PEER_KEYS + half + s * SUBLANES, a * PEER_KEYS + half + (s + 1) * SUBLANES)
                wa_ref[lo, tok] = (pltpu.bitcast(wbits << 16, F32) * gelu2(act_ref[lo, tok])).astype(BF16)
                wa_ref[hi, tok] = (pltpu.bitcast(wbits & jnp.uint32(0xFFFF0000), F32)
                                   * gelu2(act_ref[hi, tok])).astype(BF16)
    for m in range(o_ref.shape[0] // PEER_MROWS):
        ms = slice(m * PEER_MROWS, (m + 1) * PEER_MROWS)
        o_ref[ms, :] += _dot(vt_ref[ms, :], wa_ref[...])


def _peer_dense(xt, u, vt, nb, e1, rb, e2, tt=512, rows=8):
    d, t = xt.shape
    et = rows * PEER_KEYS
    tok_blk = pl.BlockSpec((d, tt), lambda i, j: (0, i))
    row_blk = pl.BlockSpec((tt // LANES, PEER_HEADS, rows, LANES), lambda i, j: (i, 0, j, 0))
    key_blk = pl.BlockSpec((PEER_HEADS, PEER_KEYS // 2, tt), lambda i, j: (0, 0, i))
    return pl.pallas_call(
        functools.partial(_peer_dense_body, rows=rows),
        grid=(t // tt, PEER_EXPERTS // et),
        in_specs=[tok_blk, pl.BlockSpec((et, d), lambda i, j: (j, 0)), pl.BlockSpec((d, et), lambda i, j: (0, j)),
                  row_blk, row_blk, key_blk, key_blk],
        out_specs=tok_blk,
        out_shape=jax.ShapeDtypeStruct((d, t), F32),
        scratch_shapes=[pltpu.VMEM((et, tt), F32), pltpu.VMEM((et, tt), BF16)],
        compiler_params=_cparams(("parallel", "arbitrary")),
        name="peer_dense",
    )(xt, u, vt, nb, e1, rb, e2)


def _ple_body(h_ref, yt_ref, g_ref, wg_ref, p_ref, wp_ref, fg_ref, o_ref, *, final):
    h = h_ref[...] + yt_ref[...].T
    gate = jax.nn.sigmoid(_dot(_rms(h, g_ref[...]).astype(BF16), wg_ref[...]))
    out = h + gate * _dot(p_ref[...].astype(BF16), wp_ref[...])
    o_ref[...] = _rms(out, fg_ref[...]) if final else out


def _ple(h, yt, g, wg, p, wp, fg, final, tm=512):
    t, d = h.shape
    full = lambda a: pl.BlockSpec(a.shape, lambda i: (0, 0))
    return pl.pallas_call(
        functools.partial(_ple_body, final=final),
        grid=(t // tm,),
        in_specs=[pl.BlockSpec((tm, d), lambda i: (i, 0)), pl.BlockSpec((d, tm), lambda i: (0, i)), full(g), full(wg),
                  pl.BlockSpec((tm, p.shape[1]), lambda i: (i, 0)), full(wp), full(fg)],
        out_specs=pl.BlockSpec((tm, d), lambda i: (i, 0)),
        out_shape=jax.ShapeDtypeStruct((t, d), F32),
        compiler_params=_cparams(("parallel",)),
        name="ple",
    )(h, yt, g, wg, p, wp, fg)


def _rot_half_cols(w):
    half = w.shape[-1] // 2
    return jnp.concatenate([-w[..., half:], w[..., :half]], axis=-1)


def _pad_cols(w, width):
    return jnp.pad(w, [(0, 0)] * (w.ndim - 1) + [(0, width - w.shape[-1])])


def _prep_w_in(w_in):
    offs = [0]
    for s in IN_SIZES:
        offs.append(offs[-1] + s)
    (d_q, d_k, d_v, g_q, g_f, g_v, g_g, m_cq, m_ckv, m_kr,
     l_q, l_k, l_v, l_i, l_f, l_o) = (w_in[..., offs[i]:offs[i + 1]] for i in range(len(IN_SIZES)))
    pieces = [d_q, d_k, d_v, g_q, g_f, g_v, g_g, l_v, l_o, m_ckv, l_q, l_k, m_cq,
              _pad_cols(m_kr, LANES), _pad_cols(_rot_half_cols(m_kr), LANES),
              _pad_cols(jnp.concatenate([l_i, l_f], axis=-1), LANES)]
    return jnp.concatenate(pieces, axis=-1).astype(BF16)


def _prep_mla(w_uq, w_ukv):
    nl = w_uq.shape[0]
    wq = w_uq.reshape(nl, MLA_Q_RANK, HEADS, MLA_NOPE + MLA_ROPE)
    rope = wq[..., MLA_NOPE:]
    wq_all = jnp.concatenate([
        wq[..., :MLA_NOPE].reshape(nl, MLA_Q_RANK, MIX_W),
        _pad_cols(rope, LANES).reshape(nl, MLA_Q_RANK, MIX_W),
        _pad_cols(_rot_half_cols(rope), LANES).reshape(nl, MLA_Q_RANK, MIX_W)], axis=-1)
    wkv = w_ukv.reshape(nl, MLA_KV_RANK, HEADS, MLA_NOPE + HEAD_V)
    wkv_all = jnp.concatenate([wkv[..., :MLA_NOPE].reshape(nl, MLA_KV_RANK, MIX_W),
                               wkv[..., MLA_NOPE:].reshape(nl, MLA_KV_RANK, MIX_W)], axis=-1)
    return wq_all.astype(BF16), wkv_all.astype(BF16)


def kernel(x, p, positions, ln_mix, w_in, diff_lambda, diff_norm, hgrn_lb_logits, hgrn_norm, mla_q_norm,
           mla_kv_norm, mla_w_uq, mla_w_ukv, mlstm_conv_w, mlstm_conv_b, mlstm_gate_b, mlstm_norm, group_gain,
           w_out, ln_ffn, peer_w_q, peer_subkeys, peer_u, peer_v, ln_ple, ple_w_gate, ple_w_proj, ln_final):
    batch, seq, d = x.shape
    depth = w_in.shape[0]
    t = batch * seq

    w_in_p = _prep_w_in(w_in)
    wq_all, wkv_all = _prep_mla(mla_w_uq, mla_w_ukv)
    w_out_b, peer_wq_b, ple_wg_b, ple_wp_b = (a.astype(BF16) for a in (w_out, peer_w_q, ple_w_gate, ple_w_proj))
    peer_u_b, peer_vt_b = peer_u.astype(BF16), jnp.swapaxes(peer_v, 1, 2).astype(BF16)
    subkeys_b = peer_subkeys.reshape(depth, PEER_HEADS * 2, PEER_KEYS, PEER_KEY_DIM).astype(BF16)

    lv = diff_lambda.astype(F32)
    lam = jnp.exp(jnp.sum(lv[:, 0] * lv[:, 1], axis=-1)) - jnp.exp(jnp.sum(lv[:, 2] * lv[:, 3], axis=-1))
    lbs = jnp.cumsum(jax.nn.softmax(hgrn_lb_logits.astype(F32), axis=0), axis=0)
    lbs = lbs - lbs[:1]
    hgrn_par = jnp.stack([jnp.maximum(jnp.log(lbs), NEG), jnp.log1p(-lbs), 1.0 - lbs, hgrn_norm], axis=1)
    hgrn_par = jnp.pad(hgrn_par, ((0, 0), (0, 4), (0, 0)))
    gate_b = _pad_cols(mlstm_gate_b.reshape(depth, 1, 2 * HEADS), LANES)
    half = MLA_ROPE // 2
    inv = ROPE_THETA ** (-jnp.arange(half, dtype=F32) / half)
    inv = _pad_cols(jnp.concatenate([inv, inv])[None, :], LANES)
    pos = positions.astype(F32).reshape(t, 1)
    row = lambda a: a.reshape(1, -1)

    h = x.reshape(t, d)
    for i in range(depth):
        lam_init = 0.8 - 0.6 * math.exp(-0.3 * i)
        z = _norm_mm(h, row(ln_mix[i]), w_in_p[i], F32, tm=512, tn=1024)
        y_diff = _diff_attn(z, (lam[i] + lam_init).reshape(1, 1), row(diff_norm[i]), batch, seq, lam_init)
        y_hgrn = _hgrn(z, hgrn_par[i], batch, seq)
        mq, mk, mv = _mla_proj(z, pos, inv, row(mla_q_norm[i]), row(mla_kv_norm[i]), wq_all[i], wkv_all[i])
        y_mla = _mla_attn(mq, mk, mv, batch, seq)
        qk = _conv_silu(z, mlstm_conv_w[i], row(mlstm_conv_b[i]), batch, seq)
        y_mlstm = _mlstm(qk, z, gate_b[i], row(mlstm_norm[i]), batch, seq)
        h = _out_proj(h, (y_diff, y_hgrn, y_mla, y_mlstm), row(group_gain[i]), w_out_b[i])
        pq, xn = _norm_mm(h, row(ln_ffn[i]), peer_wq_b[i], BF16, tm=512, tn=2048, emit_xn=True)
        yt = _peer_dense(xn.T, peer_u_b[i], peer_vt_b[i], *_peer_router(pq, subkeys_b[i]))
        h = _ple(h, yt, row(ln_ple[i]), ple_wg_b[i], p[i].reshape(t, PLE_DIM), ple_wp_b[i], row(ln_final),
                 final=(i == depth - 1))
    return h.reshape(batch, seq, d)
```

```python
import functools
import math

import jax
import jax.numpy as jnp
from jax import lax
from jax.experimental import pallas as pl
from jax.experimental.pallas import tpu as pltpu

F32 = jnp.float32
BF16 = jnp.bfloat16

D_MODEL = 2048
PLE_DIM = 256
RMS_EPS = 1e-6
ROPE_THETA = 10000.0
HEADS = 4
HEAD_V = 128
MIX_W = HEADS * HEAD_V
DIFF_QK = 64
HGRN_DK = 128
MLA_NOPE = 128
MLA_ROPE = 64
MLA_Q_RANK = 384
MLA_KV_RANK = 256
MLA_QK_PAD = 256
MLSTM_DQK = 64
MLSTM_CONV = 4
CHUNK = 64
SUB = 16
PEER_HEADS = 8
PEER_KEYS = 128
PEER_KEY_DIM = 128
PEER_TOPK = 16
PEER_EXPERTS = PEER_KEYS * PEER_KEYS
NEG = -1e30

VMEM_LIMIT = 56 * 1024 * 1024
LANES = 128
SUBLANES = 8

IN_SIZES = (512, 512, 512, 512, 512, 512, 512, 384, 256, 64, 256, 256, 512, 4, 4, 512)
C_DQ, C_DK, C_DV, C_GQ, C_GF, C_GV, C_GG, C_LV, C_LO = (i * 512 for i in range(9))
C_CKV, C_LQ, C_LK = 4608, 4864, 5120
C_CQ = 5376
C_KR, C_KRR, C_GATE = 5760, 5888, 6016
N_IN = 6144


def _cparams(sem):
    return pltpu.CompilerParams(dimension_semantics=sem, vmem_limit_bytes=VMEM_LIMIT)


def _rms(x, g):
    return x * lax.rsqrt(jnp.mean(x * x, axis=-1, keepdims=True) + RMS_EPS) * g


def _dot(a, b):
    return jnp.dot(a, b, preferred_element_type=F32)


def _dot_nt(a, b):
    return lax.dot_general(a, b, (((1,), (1,)), ((), ())), preferred_element_type=F32)


def _dot_tn(a, b):
    return lax.dot_general(a, b, (((0,), (0,)), ((), ())), preferred_element_type=F32)


def _log_sigmoid(x):
    return jnp.minimum(x, 0.0) - jnp.log1p(jnp.exp(-jnp.abs(x)))


def _cumsum_rows(x):
    row = lax.broadcasted_iota(jnp.int32, x.shape, 0)
    k = 1
    while k < x.shape[0]:
        x = x + jnp.where(row >= k, pltpu.roll(x, k, 0), 0.0)
        k *= 2
    return x


def _norm_mm_body(x_ref, g_ref, w_ref, o_ref, *rest, emit_xn):
    xn_ref = rest[-1]

    @pl.when(pl.program_id(1) == 0)
    def _():
        y = _rms(x_ref[...], g_ref[...]).astype(BF16)
        xn_ref[...] = y
        if emit_xn:
            rest[0][...] = y

    o_ref[...] = _dot(xn_ref[...], w_ref[...]).astype(o_ref.dtype)


def _norm_mm(x, g, w, out_dtype, tm, tn, emit_xn=False):
    t, d = x.shape
    n = w.shape[1]
    out_shape = [jax.ShapeDtypeStruct((t, n), out_dtype)]
    out_specs = [pl.BlockSpec((tm, tn), lambda i, j: (i, j))]
    if emit_xn:
        out_shape.append(jax.ShapeDtypeStruct((t, d), BF16))
        out_specs.append(pl.BlockSpec((tm, d), lambda i, j: (i, 0)))
    res = pl.pallas_call(
        functools.partial(_norm_mm_body, emit_xn=emit_xn),
        grid=(t // tm, n // tn),
        in_specs=[pl.BlockSpec((tm, d), lambda i, j: (i, 0)),
                  pl.BlockSpec((1, d), lambda i, j: (0, 0)),
                  pl.BlockSpec((d, tn), lambda i, j: (0, j))],
        out_specs=out_specs,
        out_shape=out_shape,
        scratch_shapes=[pltpu.VMEM((tm, d), BF16)],
        compiler_params=_cparams(("parallel", "arbitrary")),
        name="norm_mm",
    )(x, g, w)
    return res if emit_xn else res[0]


def _causal_mask(i, tq):
    row = i * tq + lax.broadcasted_iota(jnp.int32, (tq, (i + 1) * tq), 0)
    col = lax.broadcasted_iota(jnp.int32, (tq, (i + 1) * tq), 1)
    return col <= row


def _softmax_terms(s, mask):
    s = jnp.where(mask, s, NEG)
    e = jnp.exp(s - jnp.max(s, axis=-1, keepdims=True))
    return e, 1.0 / jnp.sum(e, axis=-1, keepdims=True)


def _per_query_block(nq, fn):
    for i in range(nq):
        pl.when(pl.program_id(2) == i)(functools.partial(fn, i))


def _diff_attn_body(lam_ref, q_ref, k_ref, v_ref, g_ref, o_ref, *, tq, nq, lam_init):
    def block(i):
        kv = (i + 1) * tq
        mask = _causal_mask(i, tq)
        q = q_ref[...] * (DIFF_QK ** -0.5)
        lane = lax.broadcasted_iota(jnp.int32, q.shape, 1)
        k = k_ref[0:kv, :].astype(BF16)
        e1, r1 = _softmax_terms(_dot_nt(jnp.where(lane < DIFF_QK, q, 0.0).astype(BF16), k), mask)
        e2, r2 = _softmax_terms(_dot_nt(jnp.where(lane >= DIFF_QK, q, 0.0).astype(BF16), k), mask)
        a = e1 * r1 - e2 * (lam_ref[0, 0] * r2)
        o = _dot(a.astype(BF16), v_ref[0:kv, :].astype(BF16))
        o_ref[...] = _rms(o, g_ref[...]) * (1.0 - lam_init)

    _per_query_block(nq, block)


def _diff_attn(z, lam, g, batch, seq, lam_init, tq=256):
    nq = seq // tq
    return pl.pallas_call(
        functools.partial(_diff_attn_body, tq=tq, nq=nq, lam_init=lam_init),
        grid=(batch, HEADS, nq),
        in_specs=[pl.BlockSpec(memory_space=pltpu.SMEM),
                  pl.BlockSpec((tq, LANES), lambda b, h, i: (b * nq + i, C_DQ // LANES + h)),
                  pl.BlockSpec((seq, LANES), lambda b, h, i: (b, C_DK // LANES + h)),
                  pl.BlockSpec((seq, LANES), lambda b, h, i: (b, C_DV // LANES + h)),
                  pl.BlockSpec((1, LANES), lambda b, h, i: (0, h))],
        out_specs=pl.BlockSpec((tq, LANES), lambda b, h, i: (b * nq + i, h)),
        out_shape=jax.ShapeDtypeStruct((batch * seq, MIX_W), F32),
        compiler_params=_cparams(("parallel", "parallel", "arbitrary")),
        name="diff_attn",
    )(lam, z, z, z, g)


def _mla_proj_body(cq_ref, ckv_ref, kr_ref, krr_ref, pos_ref, inv_ref, qn_ref, kvn_ref,
                   wq_ref, wkv_ref, q_ref, k_ref, v_ref):
    qa = _dot(_rms(cq_ref[...], qn_ref[...]).astype(BF16), wq_ref[...])
    kva = _dot(_rms(ckv_ref[...], kvn_ref[...]).astype(BF16), wkv_ref[...])
    ang = pos_ref[...] * inv_ref[...]
    cos, sin = jnp.cos(ang), jnp.sin(ang)
    k_rope = (kr_ref[...] * cos + krr_ref[...] * sin).astype(BF16)
    scale = (MLA_NOPE + MLA_ROPE) ** -0.5
    for h in range(HEADS):
        lo, hi = h * LANES, (h + 1) * LANES
        q_rope = qa[:, MIX_W + lo:MIX_W + hi] * cos + qa[:, 2 * MIX_W + lo:2 * MIX_W + hi] * sin
        q_ref[:, 2 * lo:2 * lo + LANES] = (qa[:, lo:hi] * scale).astype(BF16)
        q_ref[:, 2 * lo + LANES:2 * hi] = (q_rope * scale).astype(BF16)
        k_ref[:, 2 * lo:2 * lo + LANES] = kva[:, lo:hi].astype(BF16)
        k_ref[:, 2 * lo + LANES:2 * hi] = k_rope
    v_ref[...] = kva[:, MIX_W:].astype(BF16)


def _mla_proj(z, pos, inv, qn, kvn, wq, wkv, tm=512):
    t = z.shape[0]
    row = lambda c, w: pl.BlockSpec((tm, w), lambda i: (i, c // w))
    full = lambda a: pl.BlockSpec(a.shape, lambda i: (0, 0))
    return pl.pallas_call(
        _mla_proj_body,
        grid=(t // tm,),
        in_specs=[row(C_CQ, MLA_Q_RANK), row(C_CKV, MLA_KV_RANK), row(C_KR, LANES), row(C_KRR, LANES),
                  pl.BlockSpec((tm, 1), lambda i: (i, 0)), full(inv), full(qn), full(kvn), full(wq), full(wkv)],
        out_specs=[pl.BlockSpec((tm, HEADS * MLA_QK_PAD), lambda i: (i, 0)),
                   pl.BlockSpec((tm, HEADS * MLA_QK_PAD), lambda i: (i, 0)),
                   pl.BlockSpec((tm, MIX_W), lambda i: (i, 0))],
        out_shape=[jax.ShapeDtypeStruct((t, HEADS * MLA_QK_PAD), BF16),
                   jax.ShapeDtypeStruct((t, HEADS * MLA_QK_PAD), BF16),
                   jax.ShapeDtypeStruct((t, MIX_W), BF16)],
        compiler_params=_cparams(("parallel",)),
        name="mla_proj",
    )(z, z, z, z, pos, inv, qn, kvn, wq, wkv)


def _mla_attn_body(q_ref, k_ref, v_ref, o_ref, *, tq, nq):
    def block(i):
        kv = (i + 1) * tq
        e, r = _softmax_terms(_dot_nt(q_ref[...], k_ref[0:kv, :]), _causal_mask(i, tq))
        o_ref[...] = _dot((e * r).astype(BF16), v_ref[0:kv, :])

    _per_query_block(nq, block)


def _mla_attn(q, k, v, batch, seq, tq=256):
    nq = seq // tq
    return pl.pallas_call(
        functools.partial(_mla_attn_body, tq=tq, nq=nq),
        grid=(batch, HEADS, nq),
        in_specs=[pl.BlockSpec((tq, MLA_QK_PAD), lambda b, h, i: (b * nq + i, h)),
                  pl.BlockSpec((seq, MLA_QK_PAD), lambda b, h, i: (b, h)),
                  pl.BlockSpec((seq, LANES), lambda b, h, i: (b, h))],
        out_specs=pl.BlockSpec((tq, LANES), lambda b, h, i: (b * nq + i, h)),
        out_shape=jax.ShapeDtypeStruct((batch * seq, MIX_W), F32),
        compiler_params=_cparams(("parallel", "parallel", "arbitrary")),
        name="mla_attn",
    )(q, k, v)


def _hgrn_body(q_ref, f_ref, v_ref, g_ref, par_ref, o_ref, st_ref):
    @pl.when(pl.program_id(1) == 0)
    def _():
        st_ref[...] = jnp.zeros_like(st_ref)

    sub_row = lax.broadcasted_iota(jnp.int32, (SUB, SUB), 0)
    sub_col = lax.broadcasted_iota(jnp.int32, (SUB, SUB), 1)
    for h in range(HEADS):
        sl = slice(h * LANES, (h + 1) * LANES)
        log_lb, log_1m_lb, one_m_lb, gain = (par_ref[r:r + 1, sl] for r in range(4))
        zf = f_ref[:, sl]
        q = q_ref[:, sl] * (HGRN_DK ** -0.5)
        v = v_ref[:, sl].astype(BF16)
        k = one_m_lb * jax.nn.sigmoid(-zf)
        lf_a, lf_b = log_lb, log_1m_lb + _log_sigmoid(zf)
        log_f = jnp.maximum(lf_a, lf_b) + jnp.log1p(jnp.exp(-jnp.abs(lf_a - lf_b)))
        b = _cumsum_rows(log_f)
        st = st_ref[h]
        o = _dot_nt((q * jnp.exp(b)).astype(BF16), st.astype(BF16))
        parts = []
        for i in range(CHUNK // SUB):
            lo = i * SUB
            qi, bi, ki, vi = q[lo:lo + SUB], b[lo:lo + SUB], k[lo:lo + SUB], v[lo:lo + SUB]
            sc = jnp.zeros((SUB, SUB), F32)
            for s in range(SUB):
                dec = jnp.exp(jnp.minimum(bi - bi[s:s + 1], 0.0))
                col = jnp.sum(qi * ki[s:s + 1] * dec, axis=-1, keepdims=True)
                sc = sc + jnp.where(sub_col == s, col, 0.0)
            sc = jnp.where(sub_col <= sub_row, sc, 0.0)
            part = _dot(sc.astype(BF16), vi)
            if i > 0:
                b0 = b[lo - 1:lo]
                qe = (qi * jnp.exp(bi - b0)).astype(BF16)
                ke = (k[:lo] * jnp.exp(b0 - b[:lo])).astype(BF16)
                part = part + _dot(_dot_nt(qe, ke).astype(BF16), v[:lo])
            parts.append(part)
        o = o + jnp.concatenate(parts, axis=0)
        b_last = b[CHUNK - 1:CHUNK]
        st_ref[h] = st * jnp.exp(b_last) + _dot_tn(v, (k * jnp.exp(b_last - b)).astype(BF16))
        o_ref[:, sl] = _rms(o, gain) * jax.nn.silu(g_ref[:, sl])


def _hgrn(z, par, batch, seq):
    nc = seq // CHUNK
    blk = lambda c: pl.BlockSpec((CHUNK, MIX_W), lambda b, i: (b * nc + i, c // MIX_W))
    return pl.pallas_call(
        _hgrn_body,
        grid=(batch, nc),
        in_specs=[blk(C_GQ), blk(C_GF), blk(C_GV), blk(C_GG), pl.BlockSpec((8, MIX_W), lambda b, i: (0, 0))],
        out_specs=pl.BlockSpec((CHUNK, MIX_W), lambda b, i: (b * nc + i, 0)),
        out_shape=jax.ShapeDtypeStruct((batch * seq, MIX_W), F32),
        scratch_shapes=[pltpu.VMEM((HEADS, HEAD_V, HGRN_DK), F32)],
        compiler_params=_cparams(("parallel", "arbitrary")),
        name="hgrn2",
    )(z, z, z, z, par)


def _conv_silu_body(x_ref, w_ref, b_ref, o_ref):
    x = x_ref[...]
    row = lax.broadcasted_iota(jnp.int32, x.shape, 0)
    y = x * w_ref[MLSTM_CONV - 1:MLSTM_CONV] + b_ref[...]
    for k in range(1, MLSTM_CONV):
        y = y + jnp.where(row >= k, pltpu.roll(x, k, 0), 0.0) * w_ref[MLSTM_CONV - 1 - k:MLSTM_CONV - k]
    o_ref[...] = jax.nn.silu(y)


def _conv_silu(z, w, b, batch, seq):
    wd = HEADS * MLSTM_DQK
    return pl.pallas_call(
        _conv_silu_body,
        grid=(batch, 2),
        in_specs=[pl.BlockSpec((seq, wd), lambda bi, j: (bi, C_LQ // wd + j)),
                  pl.BlockSpec((MLSTM_CONV, wd), lambda bi, j: (0, j)),
                  pl.BlockSpec((1, wd), lambda bi, j: (0, j))],
        out_specs=pl.BlockSpec((seq, wd), lambda bi, j: (bi, j)),
        out_shape=jax.ShapeDtypeStruct((batch * seq, 2 * wd), F32),
        compiler_params=_cparams(("parallel", "parallel")),
        name="conv_silu",
    )(z, w, b)


def _mlstm_body(qk_ref, v_ref, gate_ref, og_ref, gb_ref, norm_ref, o_ref, c_ref, m_ref):
    @pl.when(pl.program_id(1) == 0)
    def _():
        c_ref[...] = jnp.zeros_like(c_ref)
        m_ref[...] = jnp.zeros_like(m_ref)

    g = gate_ref[...] + gb_ref[...]
    lane = lax.broadcasted_iota(jnp.int32, g.shape, 1)
    gt = jnp.where(lane < HEADS, g, _cumsum_rows(_log_sigmoid(g)))
    gt_t = gt.T
    row = lax.broadcasted_iota(jnp.int32, (CHUNK, CHUNK), 0)
    col = lax.broadcasted_iota(jnp.int32, (CHUNK, CHUNK), 1)
    ones_col = jnp.where(lane == 0, 1.0, 0.0).astype(BF16)
    half_row = lax.broadcasted_iota(jnp.int32, (LANES, 1), 0) < MLSTM_DQK
    for p in range(HEADS // 2):
        q_t = qk_ref[:, p * LANES:(p + 1) * LANES] * (MLSTM_DQK ** -0.5)
        k_t = qk_ref[:, HEADS * MLSTM_DQK + p * LANES:HEADS * MLSTM_DQK + (p + 1) * LANES]
        c_old = c_ref[p]
        c_bf = c_old.astype(BF16)
        upd = jnp.zeros_like(c_old)
        carries = []
        for u in range(2):
            h = 2 * p + u
            mine = (lane >= u * MLSTM_DQK) & (lane < (u + 1) * MLSTM_DQK)
            li_col, b_col = gt[:, h:h + 1], gt[:, HEADS + h:HEADS + h + 1]
            li_row, b_row = gt_t[h:h + 1, :], gt_t[HEADS + h:HEADS + h + 1, :]
            m_st = m_ref[h:h + 1, 0:1]
            dmat = jnp.where(col <= row, b_col - b_row + li_row, NEG)
            inter_log = b_col + m_st
            m_t = jnp.maximum(inter_log, jnp.max(dmat, axis=-1, keepdims=True))
            qm = jnp.where(mine, q_t, 0.0).astype(BF16)
            s = _dot_nt(qm, k_t.astype(BF16)) * jnp.exp(dmat - m_t)
            v_aug = jnp.concatenate([v_ref[:, h * LANES:(h + 1) * LANES].astype(BF16), ones_col], axis=1)
            out = jnp.exp(inter_log - m_t) * _dot(qm, c_bf) + _dot(s.astype(BF16), v_aug)
            hid = out[:, :HEAD_V] / jnp.maximum(jnp.abs(out[:, HEAD_V:HEAD_V + 1]), jnp.exp(-m_t))
            sl = slice(h * LANES, (h + 1) * LANES)
            o_ref[:, sl] = _rms(hid, norm_ref[:, sl]) * jax.nn.sigmoid(og_ref[:, sl])
            b_last = b_col[CHUNK - 1:CHUNK]
            log_s = b_last - b_col + li_col
            m_new = jnp.maximum(b_last + m_st, jnp.max(log_s, axis=0, keepdims=True))
            carries.append(jnp.exp(b_last + m_st - m_new))
            kw = (jnp.where(mine, k_t, 0.0) * jnp.exp(log_s - m_new)).astype(BF16)
            upd = upd + _dot_tn(kw, v_aug)
            m_ref[h:h + 1, :] = jnp.broadcast_to(m_new, (1, LANES))
        c_ref[p] = jnp.where(half_row, carries[0], carries[1]) * c_old + upd


def _mlstm(qk, z, gb, norm, batch, seq):
    nc = seq // CHUNK
    return pl.pallas_call(
        _mlstm_body,
        grid=(batch, nc),
        in_specs=[pl.BlockSpec((CHUNK, 2 * HEADS * MLSTM_DQK), lambda b, i: (b * nc + i, 0)),
                  pl.BlockSpec((CHUNK, MIX_W), lambda b, i: (b * nc + i, C_LV // MIX_W)),
                  pl.BlockSpec((CHUNK, LANES), lambda b, i: (b * nc + i, C_GATE // LANES)),
                  pl.BlockSpec((CHUNK, MIX_W), lambda b, i: (b * nc + i, C_LO // MIX_W)),
                  pl.BlockSpec((1, LANES), lambda b, i: (0, 0)),
                  pl.BlockSpec((1, MIX_W), lambda b, i: (0, 0))],
        out_specs=pl.BlockSpec((CHUNK, MIX_W), lambda b, i: (b * nc + i, 0)),
        out_shape=jax.ShapeDtypeStruct((batch * seq, MIX_W), F32),
        scratch_shapes=[pltpu.VMEM((HEADS // 2, LANES, 2 * HEAD_V), F32), pltpu.VMEM((8, LANES), F32)],
        compiler_params=_cparams(("parallel", "arbitrary")),
        name="mlstm",
    )(qk, z, z, z, gb, norm)


def _out_proj_body(h_ref, y0_ref, y1_ref, y2_ref, y3_ref, g_ref, w_ref, o_ref):
    acc = h_ref[...]
    for i, y_ref in enumerate((y0_ref, y1_ref, y2_ref, y3_ref)):
        sl = slice(i * MIX_W, (i + 1) * MIX_W)
        acc = acc + _dot((y_ref[...] * g_ref[:, sl]).astype(BF16), w_ref[sl, :])
    o_ref[...] = acc


def _out_proj(h, ys, g, w, tm=512):
    t, d = h.shape
    yspec = pl.BlockSpec((tm, MIX_W), lambda i: (i, 0))
    return pl.pallas_call(
        _out_proj_body,
        grid=(t // tm,),
        in_specs=[pl.BlockSpec((tm, d), lambda i: (i, 0)), yspec, yspec, yspec, yspec,
                  pl.BlockSpec((1, HEADS * MIX_W), lambda i: (0, 0)),
                  pl.BlockSpec((HEADS * MIX_W, d), lambda i: (0, 0))],
        out_specs=pl.BlockSpec((tm, d), lambda i: (i, 0)),
        out_shape=jax.ShapeDtypeStruct((t, d), F32),
        compiler_params=_cparams(("parallel",)),
        name="out_proj",
    )(h, *ys, g, w)


def _ranked_top(x, n, dst_ref):
    rank = jnp.full(x.shape, float(n), F32)
    for r in range(n):
        mx = jnp.max(x, axis=0, keepdims=True)
        dst_ref[r:r + 1, :] = mx
        eq = x == mx
        rank = jnp.where(eq, float(r), rank)
        x = jnp.where(eq, NEG, x)
    return rank


def _kth_and_next(x, k):
    cum = jnp.zeros((1, x.shape[1]), F32)
    kth = jnp.full((1, x.shape[1]), NEG, F32)
    nxt = kth
    for _ in range(k + 1):
        mx = jnp.max(x, axis=0, keepdims=True)
        eq = x == mx
        new = cum + jnp.sum(eq.astype(F32), axis=0, keepdims=True)
        kth = jnp.where((cum < k) & (new >= k), mx, kth)
        nxt = jnp.where((cum < k + 1) & (new >= k + 1), mx, nxt)
        cum = new
        x = jnp.where(eq, NEG, x)
    return kth, nxt


PAIR_COLS = tuple((PEER_TOPK + 1) // (i + 1) for i in range(PEER_TOPK + 1))
PAIR_OFFS = tuple(sum(PAIR_COLS[:i]) for i in range(PEER_TOPK + 2))
PAIR_ROWS = -(-PAIR_OFFS[-1] // SUBLANES) * SUBLANES


def _pack_halves(x):
    bits = pltpu.bitcast(x.astype(BF16).astype(F32), jnp.uint32)
    half = x.shape[0] // 2
    return (bits[half:] & jnp.uint32(0xFFFF0000)) | (bits[:half] >> 16)


def _peer_router_body(q_ref, sk_ref, nb_ref, e1_ref, rb_ref, e2_ref, a_ref, b_ref, g_ref):
    tt = q_ref.shape[0]
    for h in range(PEER_HEADS):
        s1 = _dot_nt(sk_ref[2 * h], q_ref[:, 2 * h * LANES:(2 * h + 1) * LANES])
        s2 = _dot_nt(sk_ref[2 * h + 1], q_ref[:, (2 * h + 1) * LANES:(2 * h + 2) * LANES])
        rank_a = _ranked_top(s1, PEER_TOPK + 1, a_ref)
        rank_b = _ranked_top(s2, PEER_TOPK + 1, b_ref)
        g_ref[PAIR_ROWS - SUBLANES:PAIR_ROWS, :] = jnp.full((SUBLANES, tt), NEG, F32)
        for i, n in enumerate(PAIR_COLS):
            g_ref[PAIR_OFFS[i]:PAIR_OFFS[i] + n, :] = a_ref[i:i + 1, :] + b_ref[0:n, :]
        pair_sums = g_ref[...]
        kth, nxt = _kth_and_next(pair_sums, PEER_TOPK)
        thr = 0.5 * (kth + nxt)
        a_top, b_top = a_ref[0:1, :], b_ref[0:1, :]
        chosen = pair_sums >= thr
        zsum = jnp.sum(jnp.where(chosen, jnp.exp(pair_sums - (a_top + b_top)), 0.0), axis=0, keepdims=True)
        ones = jnp.where(chosen, 1.0, 0.0)
        count_a = jnp.zeros_like(s1)
        for i in range(PEER_TOPK):
            n_i = jnp.sum(ones[PAIR_OFFS[i]:PAIR_OFFS[i + 1]], axis=0, keepdims=True)
            count_a = jnp.where(rank_a == float(i), n_i, count_a)
        e1 = jnp.exp(s1 - a_top).astype(BF16).astype(F32)
        for q in range(tt // LANES):
            nb_ref[q, h] = count_a[:, q * LANES:(q + 1) * LANES]
            e1_ref[q, h] = e1[:, q * LANES:(q + 1) * LANES]
        rb_ref[h] = _pack_halves(rank_b)
        e2_ref[h] = _pack_halves(jnp.exp(s2 - b_top) * (0.5 / zsum))


def _peer_router(q, sk, tt=256):
    t = q.shape[0]
    row_spec = pl.BlockSpec((tt // LANES, PEER_HEADS, PEER_KEYS, LANES), lambda i: (i, 0, 0, 0))
    row_shape = jax.ShapeDtypeStruct((t // LANES, PEER_HEADS, PEER_KEYS, LANES), F32)
    key_spec = pl.BlockSpec((PEER_HEADS, PEER_KEYS // 2, tt), lambda i: (0, 0, i))
    key_shape = jax.ShapeDtypeStruct((PEER_HEADS, PEER_KEYS // 2, t), jnp.uint32)
    return pl.pallas_call(
        _peer_router_body,
        grid=(t // tt,),
        in_specs=[pl.BlockSpec((tt, q.shape[1]), lambda i: (i, 0)),
                  pl.BlockSpec(sk.shape, lambda i: (0, 0, 0))],
        out_specs=[row_spec, row_spec, key_spec, key_spec],
        out_shape=[row_shape, row_shape, key_shape, key_shape],
        scratch_shapes=[pltpu.VMEM((24, tt), F32), pltpu.VMEM((24, tt), F32),
                        pltpu.VMEM((PAIR_ROWS, tt), F32)],
        compiler_params=_cparams(("parallel",)),
        name="peer_router",
    )(q, sk)


PEER_MROWS = 256


def _peer_dense_body(xt_ref, u_ref, vt_ref, nb_ref, e1_ref, rb_ref, e2_ref, o_ref, act_ref, wa_ref, *, rows):
    @pl.when(pl.program_id(1) == 0)
    def _():
        o_ref[...] = jnp.zeros_like(o_ref)

    et, tt = act_ref.shape
    half = PEER_KEYS // 2
    for m in range(et // PEER_MROWS):
        ms = slice(m * PEER_MROWS, (m + 1) * PEER_MROWS)
        act_ref[ms, :] = _dot(u_ref[ms, :], xt_ref[...])

    def gelu2(z):
        return z + z * lax.erf(z * (2.0 ** -0.5))

    for a in range(rows):
        for q in range(tt // LANES):
            tok = slice(q * LANES, (q + 1) * LANES)

            def row(ref, h):
                bits = pltpu.bitcast(jnp.broadcast_to(ref[q, h, a:a + 1, :], (SUBLANES, LANES)), jnp.uint32)
                return pltpu.bitcast((bits & jnp.uint32(0xFFFF0000)) | (bits >> 16), BF16)

            nbs = [row(nb_ref, h) for h in range(PEER_HEADS)]
            e1s = [row(e1_ref, h) for h in range(PEER_HEADS)]
            for s in range(half // SUBLANES):
                ks = slice(s * SUBLANES, (s + 1) * SUBLANES)
                w = None
                for h in range(PEER_HEADS):
                    rb = pltpu.bitcast(rb_ref[h, ks, tok], BF16)
                    e2 = pltpu.bitcast(e2_ref[h, ks, tok], BF16)
                    t = jnp.where(rb < nbs[h], e2 * e1s[h], 0.0)
                    w = t if w is None else w + t
                wbits = pltpu.bitcast(w, jnp.uint32)
                lo = slice(a * PEER_KEYS + s * SUBLANES, a * PEER_KEYS + (s + 1) * SUBLANES)
                hi = slice(a * PEER_KEYS + half + s * SUBLANES, a * PEER_KEYS + half + (s + 1) * SUBLANES)
                wa_ref[lo, tok] = (pltpu.bitcast(wbits << 16, F32) * gelu2(act_ref[lo, tok])).astype(BF16)
                wa_ref[hi, tok] = (pltpu.bitcast(wbits & jnp.uint32(0xFFFF0000), F32)
                                   * gelu2(act_ref[hi, tok])).astype(BF16)
    for m in range(o_ref.shape[0] // PEER_MROWS):
        ms = slice(m * PEER_MROWS, (m + 1) * PEER_MROWS)
        o_ref[ms, :] += _dot(vt_ref[ms, :], wa_ref[...])


def _peer_dense(xt, u, vt, nb, e1, rb, e2, tt=512, rows=8):
    d, t = xt.shape
    et = rows * PEER_KEYS
    tok_blk = pl.BlockSpec((d, tt), lambda i, j: (0, i))
    row_blk = pl.BlockSpec((tt // LANES, PEER_HEADS, rows, LANES), lambda i, j: (i, 0, j, 0))
    key_blk = pl.BlockSpec((PEER_HEADS, PEER_KEYS // 2, tt), lambda i, j: (0, 0, i))
    return pl.pallas_call(
        functools.partial(_peer_dense_body, rows=rows),
        grid=(t // tt, PEER_EXPERTS // et),
        in_specs=[tok_blk, pl.BlockSpec((et, d), lambda i, j: (j, 0)), pl.BlockSpec((d, et), lambda i, j: (0, j)),
                  row_blk, row_blk, key_blk, key_blk],
        out_specs=tok_blk,
        out_shape=jax.ShapeDtypeStruct((d, t), F32),
        scratch_shapes=[pltpu.VMEM((et, tt), F32), pltpu.VMEM((et, tt), BF16)],
        compiler_params=_cparams(("parallel", "arbitrary")),
        name="peer_dense",
    )(xt, u, vt, nb, e1, rb, e2)


def _ple_body(h_ref, yt_ref, g_ref, wg_ref, p_ref, wp_ref, fg_ref, o_ref, *, final):
    h = h_ref[...] + yt_ref[...].T
    gate = jax.nn.sigmoid(_dot(_rms(h, g_ref[...]).astype(BF16), wg_ref[...]))
    out = h + gate * _dot(p_ref[...].astype(BF16), wp_ref[...])
    o_ref[...] = _rms(out, fg_ref[...]) if final else out


def _ple(h, yt, g, wg, p, wp, fg, final, tm=512):
    t, d = h.shape
    full = lambda a: pl.BlockSpec(a.shape, lambda i: (0, 0))
    return pl.pallas_call(
        functools.partial(_ple_body, final=final),
        grid=(t // tm,),
        in_specs=[pl.BlockSpec((tm, d), lambda i: (i, 0)), pl.BlockSpec((d, tm), lambda i: (0, i)), full(g), full(wg),
                  pl.BlockSpec((tm, p.shape[1]), lambda i: (i, 0)), full(wp), full(fg)],
        out_specs=pl.BlockSpec((tm, d), lambda i: (i, 0)),
        out_shape=jax.ShapeDtypeStruct((t, d), F32),
        compiler_params=_cparams(("parallel",)),
        name="ple",
    )(h, yt, g, wg, p, wp, fg)


def _cast_transpose_body(x_ref, o_ref):
    o_ref[...] = x_ref[...].T.astype(o_ref.dtype)


def _cast_transpose(x, dtype, tile=1024):
    nl, r, c = x.shape
    return pl.pallas_call(
        _cast_transpose_body,
        grid=(nl, r // tile, c // tile),
        in_specs=[pl.BlockSpec((None, tile, tile), lambda l, i, j: (l, i, j))],
        out_specs=pl.BlockSpec((None, tile, tile), lambda l, i, j: (l, j, i)),
        out_shape=jax.ShapeDtypeStruct((nl, c, r), dtype),
        compiler_params=_cparams(("parallel", "parallel", "parallel")),
        name="cast_transpose",
    )(x)


def _rot_half_cols(w):
    half = w.shape[-1] // 2
    return jnp.concatenate([-w[..., half:], w[..., :half]], axis=-1)


def _pad_cols(w, width):
    return jnp.pad(w, [(0, 0)] * (w.ndim - 1) + [(0, width - w.shape[-1])])


def _prep_w_in(w_in):
    offs = [0]
    for s in IN_SIZES:
        offs.append(offs[-1] + s)
    (d_q, d_k, d_v, g_q, g_f, g_v, g_g, m_cq, m_ckv, m_kr,
     l_q, l_k, l_v, l_i, l_f, l_o) = (w_in[..., offs[i]:offs[i + 1]] for i in range(len(IN_SIZES)))
    pieces = [d_q, d_k, d_v, g_q, g_f, g_v, g_g, l_v, l_o, m_ckv, l_q, l_k, m_cq,
              _pad_cols(m_kr, LANES), _pad_cols(_rot_half_cols(m_kr), LANES),
              _pad_cols(jnp.concatenate([l_i, l_f], axis=-1), LANES)]
    return jnp.concatenate(pieces, axis=-1).astype(BF16)


def _prep_mla(w_uq, w_ukv):
    nl = w_uq.shape[0]
    wq = w_uq.reshape(nl, MLA_Q_RANK, HEADS, MLA_NOPE + MLA_ROPE)
    rope = wq[..., MLA_NOPE:]
    wq_all = jnp.concatenate([
        wq[..., :MLA_NOPE].reshape(nl, MLA_Q_RANK, MIX_W),
        _pad_cols(rope, LANES).reshape(nl, MLA_Q_RANK, MIX_W),
        _pad_cols(_rot_half_cols(rope), LANES).reshape(nl, MLA_Q_RANK, MIX_W)], axis=-1)
    wkv = w_ukv.reshape(nl, MLA_KV_RANK, HEADS, MLA_NOPE + HEAD_V)
    wkv_all = jnp.concatenate([wkv[..., :MLA_NOPE].reshape(nl, MLA_KV_RANK, MIX_W),
                               wkv[..., MLA_NOPE:].reshape(nl, MLA_KV_RANK, MIX_W)], axis=-1)
    return wq_all.astype(BF16), wkv_all.astype(BF16)


def kernel(x, p, positions, ln_mix, w_in, diff_lambda, diff_norm, hgrn_lb_logits, hgrn_norm, mla_q_norm,
           mla_kv_norm, mla_w_uq, mla_w_ukv, mlstm_conv_w, mlstm_conv_b, mlstm_gate_b, mlstm_norm, group_gain,
           w_out, ln_ffn, peer_w_q, peer_subkeys, peer_u, peer_v, ln_ple, ple_w_gate, ple_w_proj, ln_final):
    batch, seq, d = x.shape
    depth = w_in.shape[0]
    t = batch * seq

    w_in_p = _prep_w_in(w_in)
    wq_all, wkv_all = _prep_mla(mla_w_uq, mla_w_ukv)
    w_out_b, peer_wq_b, ple_wg_b, ple_wp_b = (a.astype(BF16) for a in (w_out, peer_w_q, ple_w_gate, ple_w_proj))
    peer_u_b, peer_vt_b = peer_u.astype(BF16), _cast_transpose(peer_v, BF16)
    subkeys_b = peer_subkeys.reshape(depth, PEER_HEADS * 2, PEER_KEYS, PEER_KEY_DIM).astype(BF16)

    lv = diff_lambda.astype(F32)
    lam = jnp.exp(jnp.sum(lv[:, 0] * lv[:, 1], axis=-1)) - jnp.exp(jnp.sum(lv[:, 2] * lv[:, 3], axis=-1))
    lbs = jnp.cumsum(jax.nn.softmax(hgrn_lb_logits.astype(F32), axis=0), axis=0)
    lbs = lbs - lbs[:1]
    hgrn_par = jnp.stack([jnp.maximum(jnp.log(lbs), NEG), jnp.log1p(-lbs), 1.0 - lbs, hgrn_norm], axis=1)
    hgrn_par = jnp.pad(hgrn_par, ((0, 0), (0, 4), (0, 0)))
    gate_b = _pad_cols(mlstm_gate_b.reshape(depth, 1, 2 * HEADS), LANES)
    half = MLA_ROPE // 2
    inv = ROPE_THETA ** (-jnp.arange(half, dtype=F32) / half)
    inv = _pad_cols(jnp.concatenate([inv, inv])[None, :], LANES)
    pos = positions.astype(F32).reshape(t, 1)
    row = lambda a: a.reshape(1, -1)

    h = x.reshape(t, d)
    for i in range(depth):
        lam_init = 0.8 - 0.6 * math.exp(-0.3 * i)
        z = _norm_mm(h, row(ln_mix[i]), w_in_p[i], F32, tm=512, tn=2048)
        y_diff = _diff_attn(z, (lam[i] + lam_init).reshape(1, 1), row(diff_norm[i]), batch, seq, lam_init)
        y_hgrn = _hgrn(z, hgrn_par[i], batch, seq)
        mq, mk, mv = _mla_proj(z, pos, inv, row(mla_q_norm[i]), row(mla_kv_norm[i]), wq_all[i], wkv_all[i])
        y_mla = _mla_attn(mq, mk, mv, batch, seq)
        qk = _conv_silu(z, mlstm_conv_w[i], row(mlstm_conv_b[i]), batch, seq)
        y_mlstm = _mlstm(qk, z, gate_b[i], row(mlstm_norm[i]), batch, seq)
        h = _out_proj(h, (y_diff, y_hgrn, y_mla, y_mlstm), row(group_gain[i]), w_out_b[i])
        pq, xn = _norm_mm(h, row(ln_ffn[i]), peer_wq_b[i], BF16, tm=512, tn=2048, emit_xn=True)
        yt = _peer_dense(xn.T, peer_u_b[i], peer_vt_b[i], *_peer_router(pq, subkeys_b[i]))
        h = _ple(h, yt, row(ln_ple[i]), ple_wg_b[i], p[i].reshape(t, PLE_DIM), ple_wp_b[i], row(ln_final),
                 final=(i == depth - 1))
    return h.reshape(batch, seq, d)
```

```python
import functools
import math

import jax
import jax.numpy as jnp
from jax import lax
from jax.experimental import pallas as pl
from jax.experimental.pallas import tpu as pltpu

F32 = jnp.float32
BF16 = jnp.bfloat16

D_MODEL = 2048
PLE_DIM = 256
RMS_EPS = 1e-6
ROPE_THETA = 10000.0
HEADS = 4
HEAD_V = 128
MIX_W = HEADS * HEAD_V
DIFF_QK = 64
HGRN_DK = 128
MLA_NOPE = 128
MLA_ROPE = 64
MLA_Q_RANK = 384
MLA_KV_RANK = 256
MLA_QK_PAD = 256
MLSTM_DQK = 64
MLSTM_CONV = 4
CHUNK = 128
HGRN_CHUNK = 64
SUB = 16
PEER_HEADS = 8
PEER_KEYS = 128
PEER_KEY_DIM = 128
PEER_TOPK = 16
PEER_EXPERTS = PEER_KEYS * PEER_KEYS
NEG = -1e30

VMEM_LIMIT = 56 * 1024 * 1024
LANES = 128
SUBLANES = 8

IN_SIZES = (512, 512, 512, 512, 512, 512, 512, 384, 256, 64, 256, 256, 512, 4, 4, 512)
C_DQ, C_DK, C_DV, C_GQ, C_GF, C_GV, C_GG, C_LV, C_LO = (i * 512 for i in range(9))
C_CKV, C_LQ, C_LK = 4608, 4864, 5120
C_CQ = 5376
C_KR, C_KRR, C_GATE = 5760, 5888, 6016
N_IN = 6144


def _cparams(sem):
    return pltpu.CompilerParams(dimension_semantics=sem, vmem_limit_bytes=VMEM_LIMIT)


def _rms(x, g):
    return x * lax.rsqrt(jnp.mean(x * x, axis=-1, keepdims=True) + RMS_EPS) * g


def _dot(a, b):
    return jnp.dot(a, b, preferred_element_type=F32)


def _dot_nt(a, b):
    return lax.dot_general(a, b, (((1,), (1,)), ((), ())), preferred_element_type=F32)


def _dot_tn(a, b):
    return lax.dot_general(a, b, (((0,), (0,)), ((), ())), preferred_element_type=F32)


def _log_sigmoid(x):
    return jnp.minimum(x, 0.0) - jnp.log1p(jnp.exp(-jnp.abs(x)))


def _cumsum_rows(x):
    row = lax.broadcasted_iota(jnp.int32, x.shape, 0)
    k = 1
    while k < x.shape[0]:
        x = x + jnp.where(row >= k, pltpu.roll(x, k, 0), 0.0)
        k *= 2
    return x


def _norm_mm_body(x_ref, g_ref, w_ref, o_ref, *rest, emit_xn):
    xn_ref = rest[-1]

    @pl.when(pl.program_id(1) == 0)
    def _():
        y = _rms(x_ref[...], g_ref[...]).astype(BF16)
        xn_ref[...] = y
        if emit_xn:
            rest[0][...] = y

    o_ref[...] = _dot(xn_ref[...], w_ref[...]).astype(o_ref.dtype)


def _norm_mm(x, g, w, out_dtype, tm, tn, emit_xn=False):
    t, d = x.shape
    n = w.shape[1]
    out_shape = [jax.ShapeDtypeStruct((t, n), out_dtype)]
    out_specs = [pl.BlockSpec((tm, tn), lambda i, j: (i, j))]
    if emit_xn:
        out_shape.append(jax.ShapeDtypeStruct((t, d), BF16))
        out_specs.append(pl.BlockSpec((tm, d), lambda i, j: (i, 0)))
    res = pl.pallas_call(
        functools.partial(_norm_mm_body, emit_xn=emit_xn),
        grid=(t // tm, n // tn),
        in_specs=[pl.BlockSpec((tm, d), lambda i, j: (i, 0)),
                  pl.BlockSpec((1, d), lambda i, j: (0, 0)),
                  pl.BlockSpec((d, tn), lambda i, j: (0, j))],
        out_specs=out_specs,
        out_shape=out_shape,
        scratch_shapes=[pltpu.VMEM((tm, d), BF16)],
        compiler_params=_cparams(("parallel", "arbitrary")),
        name="norm_mm",
    )(x, g, w)
    return res if emit_xn else res[0]


def _causal_mask(i, tq):
    row = i * tq + lax.broadcasted_iota(jnp.int32, (tq, (i + 1) * tq), 0)
    col = lax.broadcasted_iota(jnp.int32, (tq, (i + 1) * tq), 1)
    return col <= row


def _softmax_terms(s, mask):
    s = jnp.where(mask, s, NEG)
    e = jnp.exp(s - jnp.max(s, axis=-1, keepdims=True))
    return e, 1.0 / jnp.sum(e, axis=-1, keepdims=True)


def _per_query_block(nq, fn):
    for i in range(nq):
        pl.when(pl.program_id(2) == i)(functools.partial(fn, i))


def _diff_attn_body(lam_ref, q_ref, k_ref, v_ref, g_ref, o_ref, *, tq, nq, lam_init):
    def block(i):
        kv = (i + 1) * tq
        mask = _causal_mask(i, tq)
        q = q_ref[...] * (DIFF_QK ** -0.5)
        lane = lax.broadcasted_iota(jnp.int32, q.shape, 1)
        k = k_ref[0:kv, :].astype(BF16)
        e1, r1 = _softmax_terms(_dot_nt(jnp.where(lane < DIFF_QK, q, 0.0).astype(BF16), k), mask)
        e2, r2 = _softmax_terms(_dot_nt(jnp.where(lane >= DIFF_QK, q, 0.0).astype(BF16), k), mask)
        a = e1 * r1 - e2 * (lam_ref[0, 0] * r2)
        o = _dot(a.astype(BF16), v_ref[0:kv, :].astype(BF16))
        o_ref[...] = _rms(o, g_ref[...]) * (1.0 - lam_init)

    _per_query_block(nq, block)


def _diff_attn(z, lam, g, batch, seq, lam_init, tq=256):
    nq = seq // tq
    return pl.pallas_call(
        functools.partial(_diff_attn_body, tq=tq, nq=nq, lam_init=lam_init),
        grid=(batch, HEADS, nq),
        in_specs=[pl.BlockSpec(memory_space=pltpu.SMEM),
                  pl.BlockSpec((tq, LANES), lambda b, h, i: (b * nq + i, C_DQ // LANES + h)),
                  pl.BlockSpec((seq, LANES), lambda b, h, i: (b, C_DK // LANES + h)),
                  pl.BlockSpec((seq, LANES), lambda b, h, i: (b, C_DV // LANES + h)),
                  pl.BlockSpec((1, LANES), lambda b, h, i: (0, h))],
        out_specs=pl.BlockSpec((tq, LANES), lambda b, h, i: (b * nq + i, h)),
        out_shape=jax.ShapeDtypeStruct((batch * seq, MIX_W), F32),
        compiler_params=_cparams(("parallel", "parallel", "arbitrary")),
        name="diff_attn",
    )(lam, z, z, z, g)


def _mla_proj_body(cq_ref, ckv_ref, kr_ref, krr_ref, pos_ref, inv_ref, qn_ref, kvn_ref,
                   wq_ref, wkv_ref, q_ref, k_ref, v_ref):
    qa = _dot(_rms(cq_ref[...], qn_ref[...]).astype(BF16), wq_ref[...])
    kva = _dot(_rms(ckv_ref[...], kvn_ref[...]).astype(BF16), wkv_ref[...])
    ang = pos_ref[...] * inv_ref[...]
    cos, sin = jnp.cos(ang), jnp.sin(ang)
    k_rope = (kr_ref[...] * cos + krr_ref[...] * sin).astype(BF16)
    scale = (MLA_NOPE + MLA_ROPE) ** -0.5
    for h in range(HEADS):
        lo, hi = h * LANES, (h + 1) * LANES
        q_rope = qa[:, MIX_W + lo:MIX_W + hi] * cos + qa[:, 2 * MIX_W + lo:2 * MIX_W + hi] * sin
        q_ref[:, 2 * lo:2 * lo + LANES] = (qa[:, lo:hi] * scale).astype(BF16)
        q_ref[:, 2 * lo + LANES:2 * hi] = (q_rope * scale).astype(BF16)
        k_ref[:, 2 * lo:2 * lo + LANES] = kva[:, lo:hi].astype(BF16)
        k_ref[:, 2 * lo + LANES:2 * hi] = k_rope
    v_ref[...] = kva[:, MIX_W:].astype(BF16)


def _mla_proj(z, pos, inv, qn, kvn, wq, wkv, tm=512):
    t = z.shape[0]
    row = lambda c, w: pl.BlockSpec((tm, w), lambda i: (i, c // w))
    full = lambda a: pl.BlockSpec(a.shape, lambda i: (0, 0))
    return pl.pallas_call(
        _mla_proj_body,
        grid=(t // tm,),
        in_specs=[row(C_CQ, MLA_Q_RANK), row(C_CKV, MLA_KV_RANK), row(C_KR, LANES), row(C_KRR, LANES),
                  pl.BlockSpec((tm, 1), lambda i: (i, 0)), full(inv), full(qn), full(kvn), full(wq), full(wkv)],
        out_specs=[pl.BlockSpec((tm, HEADS * MLA_QK_PAD), lambda i: (i, 0)),
                   pl.BlockSpec((tm, HEADS * MLA_QK_PAD), lambda i: (i, 0)),
                   pl.BlockSpec((tm, MIX_W), lambda i: (i, 0))],
        out_shape=[jax.ShapeDtypeStruct((t, HEADS * MLA_QK_PAD), BF16),
                   jax.ShapeDtypeStruct((t, HEADS * MLA_QK_PAD), BF16),
                   jax.ShapeDtypeStruct((t, MIX_W), BF16)],
        compiler_params=_cparams(("parallel",)),
        name="mla_proj",
    )(z, z, z, z, pos, inv, qn, kvn, wq, wkv)


def _mla_attn_body(q_ref, k_ref, v_ref, o_ref, *, tq, nq):
    def block(i):
        kv = (i + 1) * tq
        e, r = _softmax_terms(_dot_nt(q_ref[...], k_ref[0:kv, :]), _causal_mask(i, tq))
        o_ref[...] = _dot((e * r).astype(BF16), v_ref[0:kv, :])

    _per_query_block(nq, block)


def _mla_attn(q, k, v, batch, seq, tq=256):
    nq = seq // tq
    return pl.pallas_call(
        functools.partial(_mla_attn_body, tq=tq, nq=nq),
        grid=(batch, HEADS, nq),
        in_specs=[pl.BlockSpec((tq, MLA_QK_PAD), lambda b, h, i: (b * nq + i, h)),
                  pl.BlockSpec((seq, MLA_QK_PAD), lambda b, h, i: (b, h)),
                  pl.BlockSpec((seq, LANES), lambda b, h, i: (b, h))],
        out_specs=pl.BlockSpec((tq, LANES), lambda b, h, i: (b * nq + i, h)),
        out_shape=jax.ShapeDtypeStruct((batch * seq, MIX_W), F32),
        compiler_params=_cparams(("parallel", "parallel", "arbitrary")),
        name="mla_attn",
    )(q, k, v)


def _hgrn_body(q_ref, f_ref, v_ref, g_ref, par_ref, o_ref, st_ref):
    @pl.when(pl.program_id(1) == 0)
    def _():
        st_ref[...] = jnp.zeros_like(st_ref)

    sub_row = lax.broadcasted_iota(jnp.int32, (SUB, SUB), 0)
    sub_col = lax.broadcasted_iota(jnp.int32, (SUB, SUB), 1)
    for h in range(HEADS):
        sl = slice(h * LANES, (h + 1) * LANES)
        log_lb, log_1m_lb, one_m_lb, gain = (par_ref[r:r + 1, sl] for r in range(4))
        zf = f_ref[:, sl]
        q = q_ref[:, sl] * (HGRN_DK ** -0.5)
        v = v_ref[:, sl].astype(BF16)
        k = one_m_lb * jax.nn.sigmoid(-zf)
        lf_a, lf_b = log_lb, log_1m_lb + _log_sigmoid(zf)
        log_f = jnp.maximum(lf_a, lf_b) + jnp.log1p(jnp.exp(-jnp.abs(lf_a - lf_b)))
        b = _cumsum_rows(log_f)
        st = st_ref[h]
        o = _dot_nt((q * jnp.exp(b)).astype(BF16), st.astype(BF16))
        parts = []
        for i in range(HGRN_CHUNK // SUB):
            lo = i * SUB
            qi, bi, ki, vi = q[lo:lo + SUB], b[lo:lo + SUB], k[lo:lo + SUB], v[lo:lo + SUB]
            sc = jnp.zeros((SUB, SUB), F32)
            for s in range(SUB):
                dec = jnp.exp(jnp.minimum(bi - bi[s:s + 1], 0.0))
                col = jnp.sum(qi * ki[s:s + 1] * dec, axis=-1, keepdims=True)
                sc = sc + jnp.where(sub_col == s, col, 0.0)
            sc = jnp.where(sub_col <= sub_row, sc, 0.0)
            part = _dot(sc.astype(BF16), vi)
            if i > 0:
                b0 = b[lo - 1:lo]
                qe = (qi * jnp.exp(bi - b0)).astype(BF16)
                ke = (k[:lo] * jnp.exp(b0 - b[:lo])).astype(BF16)
                part = part + _dot(_dot_nt(qe, ke).astype(BF16), v[:lo])
            parts.append(part)
        o = o + jnp.concatenate(parts, axis=0)
        b_last = b[HGRN_CHUNK - 1:HGRN_CHUNK]
        st_ref[h] = st * jnp.exp(b_last) + _dot_tn(v, (k * jnp.exp(b_last - b)).astype(BF16))
        o_ref[:, sl] = _rms(o, gain) * jax.nn.silu(g_ref[:, sl])


def _hgrn(z, par, batch, seq):
    nc = seq // HGRN_CHUNK
    blk = lambda c: pl.BlockSpec((HGRN_CHUNK, MIX_W), lambda b, i: (b * nc + i, c // MIX_W))
    return pl.pallas_call(
        _hgrn_body,
        grid=(batch, nc),
        in_specs=[blk(C_GQ), blk(C_GF), blk(C_GV), blk(C_GG), pl.BlockSpec((8, MIX_W), lambda b, i: (0, 0))],
        out_specs=pl.BlockSpec((HGRN_CHUNK, MIX_W), lambda b, i: (b * nc + i, 0)),
        out_shape=jax.ShapeDtypeStruct((batch * seq, MIX_W), F32),
        scratch_shapes=[pltpu.VMEM((HEADS, HEAD_V, HGRN_DK), F32)],
        compiler_params=_cparams(("parallel", "arbitrary")),
        name="hgrn2",
    )(z, z, z, z, par)


def _conv_silu_body(x_ref, w_ref, b_ref, o_ref):
    x = x_ref[...]
    row = lax.broadcasted_iota(jnp.int32, x.shape, 0)
    y = x * w_ref[MLSTM_CONV - 1:MLSTM_CONV] + b_ref[...]
    for k in range(1, MLSTM_CONV):
        y = y + jnp.where(row >= k, pltpu.roll(x, k, 0), 0.0) * w_ref[MLSTM_CONV - 1 - k:MLSTM_CONV - k]
    o_ref[...] = jax.nn.silu(y)


def _conv_silu(z, w, b, batch, seq):
    wd = HEADS * MLSTM_DQK
    return pl.pallas_call(
        _conv_silu_body,
        grid=(batch, 2),
        in_specs=[pl.BlockSpec((seq, wd), lambda bi, j: (bi, C_LQ // wd + j)),
                  pl.BlockSpec((MLSTM_CONV, wd), lambda bi, j: (0, j)),
                  pl.BlockSpec((1, wd), lambda bi, j: (0, j))],
        out_specs=pl.BlockSpec((seq, wd), lambda bi, j: (bi, j)),
        out_shape=jax.ShapeDtypeStruct((batch * seq, 2 * wd), F32),
        compiler_params=_cparams(("parallel", "parallel")),
        name="conv_silu",
    )(z, w, b)


def _mlstm_body(qk_ref, v_ref, gate_ref, og_ref, gb_ref, norm_ref, o_ref, c_ref, m_ref):
    @pl.when(pl.program_id(1) == 0)
    def _():
        c_ref[...] = jnp.zeros_like(c_ref)
        m_ref[...] = jnp.zeros_like(m_ref)

    g = gate_ref[...] + gb_ref[...]
    lane = lax.broadcasted_iota(jnp.int32, g.shape, 1)
    gt = jnp.where(lane < HEADS, g, _cumsum_rows(_log_sigmoid(g)))
    gt_t = gt.T
    row = lax.broadcasted_iota(jnp.int32, (CHUNK, CHUNK), 0)
    col = lax.broadcasted_iota(jnp.int32, (CHUNK, CHUNK), 1)
    ones_col = jnp.where(lane == 0, 1.0, 0.0).astype(BF16)
    half_row = lax.broadcasted_iota(jnp.int32, (LANES, 1), 0) < MLSTM_DQK
    for p in range(HEADS // 2):
        q_t = qk_ref[:, p * LANES:(p + 1) * LANES] * (MLSTM_DQK ** -0.5)
        k_t = qk_ref[:, HEADS * MLSTM_DQK + p * LANES:HEADS * MLSTM_DQK + (p + 1) * LANES]
        c_old = c_ref[p]
        c_bf = c_old.astype(BF16)
        upd = jnp.zeros_like(c_old)
        carries = []
        for u in range(2):
            h = 2 * p + u
            mine = (lane >= u * MLSTM_DQK) & (lane < (u + 1) * MLSTM_DQK)
            li_col, b_col = gt[:, h:h + 1], gt[:, HEADS + h:HEADS + h + 1]
            li_row, b_row = gt_t[h:h + 1, :], gt_t[HEADS + h:HEADS + h + 1, :]
            m_st = m_ref[h:h + 1, 0:1]
            dmat = jnp.where(col <= row, b_col - b_row + li_row, NEG)
            inter_log = b_col + m_st
            m_t = jnp.maximum(inter_log, jnp.max(dmat, axis=-1, keepdims=True))
            qm = jnp.where(mine, q_t, 0.0).astype(BF16)
            s = _dot_nt(qm, k_t.astype(BF16)) * jnp.exp(dmat - m_t)
            v_aug = jnp.concatenate([v_ref[:, h * LANES:(h + 1) * LANES].astype(BF16), ones_col], axis=1)
            out = jnp.exp(inter_log - m_t) * _dot(qm, c_bf) + _dot(s.astype(BF16), v_aug)
            hid = out[:, :HEAD_V] / jnp.maximum(jnp.abs(out[:, HEAD_V:HEAD_V + 1]), jnp.exp(-m_t))
            sl = slice(h * LANES, (h + 1) * LANES)
            o_ref[:, sl] = _rms(hid, norm_ref[:, sl]) * jax.nn.sigmoid(og_ref[:, sl])
            b_last = b_col[CHUNK - 1:CHUNK]
            log_s = b_last - b_col + li_col
            m_new = jnp.maximum(b_last + m_st, jnp.max(log_s, axis=0, keepdims=True))
            carries.append(jnp.exp(b_last + m_st - m_new))
            kw = (jnp.where(mine, k_t, 0.0) * jnp.exp(log_s - m_new)).astype(BF16)
            upd = upd + _dot_tn(kw, v_aug)
            m_ref[h:h + 1, :] = jnp.broadcast_to(m_new, (1, LANES))
        c_ref[p] = jnp.where(half_row, carries[0], carries[1]) * c_old + upd


def _mlstm(qk, z, gb, norm, batch, seq):
    nc = seq // CHUNK
    return pl.pallas_call(
        _mlstm_body,
        grid=(batch, nc),
        in_specs=[pl.BlockSpec((CHUNK, 2 * HEADS * MLSTM_DQK), lambda b, i: (b * nc + i, 0)),
                  pl.BlockSpec((CHUNK, MIX_W), lambda b, i: (b * nc + i, C_LV // MIX_W)),
                  pl.BlockSpec((CHUNK, LANES), lambda b, i: (b * nc + i, C_GATE // LANES)),
                  pl.BlockSpec((CHUNK, MIX_W), lambda b, i: (b * nc + i, C_LO // MIX_W)),
                  pl.BlockSpec((1, LANES), lambda b, i: (0, 0)),
                  pl.BlockSpec((1, MIX_W), lambda b, i: (0, 0))],
        out_specs=pl.BlockSpec((CHUNK, MIX_W), lambda b, i: (b * nc + i, 0)),
        out_shape=jax.ShapeDtypeStruct((batch * seq, MIX_W), F32),
        scratch_shapes=[pltpu.VMEM((HEADS // 2, LANES, 2 * HEAD_V), F32), pltpu.VMEM((8, LANES), F32)],
        compiler_params=_cparams(("parallel", "arbitrary")),
        name="mlstm",
    )(qk, z, z, z, gb, norm)


def _out_proj_body(h_ref, y0_ref, y1_ref, y2_ref, y3_ref, g_ref, w_ref, o_ref):
    acc = h_ref[...]
    for i, y_ref in enumerate((y0_ref, y1_ref, y2_ref, y3_ref)):
        sl = slice(i * MIX_W, (i + 1) * MIX_W)
        acc = acc + _dot((y_ref[...] * g_ref[:, sl]).astype(BF16), w_ref[sl, :])
    o_ref[...] = acc


def _out_proj(h, ys, g, w, tm=512):
    t, d = h.shape
    yspec = pl.BlockSpec((tm, MIX_W), lambda i: (i, 0))
    return pl.pallas_call(
        _out_proj_body,
        grid=(t // tm,),
        in_specs=[pl.BlockSpec((tm, d), lambda i: (i, 0)), yspec, yspec, yspec, yspec,
                  pl.BlockSpec((1, HEADS * MIX_W), lambda i: (0, 0)),
                  pl.BlockSpec((HEADS * MIX_W, d), lambda i: (0, 0))],
        out_specs=pl.BlockSpec((tm, d), lambda i: (i, 0)),
        out_shape=jax.ShapeDtypeStruct((t, d), F32),
        compiler_params=_cparams(("parallel",)),
        name="out_proj",
    )(h, *ys, g, w)


def _ranked_top(x, n, dst_ref):
    rank = jnp.full(x.shape, float(n), F32)
    for r in range(n):
        mx = jnp.max(x, axis=0, keepdims=True)
        dst_ref[r:r + 1, :] = mx
        eq = x == mx
        rank = jnp.where(eq, float(r), rank)
        x = jnp.where(eq, NEG, x)
    return rank


def _kth_and_next(x, k):
    cum = jnp.zeros((1, x.shape[1]), F32)
    kth = jnp.full((1, x.shape[1]), NEG, F32)
    nxt = kth
    for _ in range(k + 1):
        mx = jnp.max(x, axis=0, keepdims=True)
        eq = x == mx
        new = cum + jnp.sum(eq.astype(F32), axis=0, keepdims=True)
        kth = jnp.where((cum < k) & (new >= k), mx, kth)
        nxt = jnp.where((cum < k + 1) & (new >= k + 1), mx, nxt)
        cum = new
        x = jnp.where(eq, NEG, x)
    return kth, nxt


PAIR_COLS = tuple((PEER_TOPK + 1) // (i + 1) for i in range(PEER_TOPK + 1))
PAIR_OFFS = tuple(sum(PAIR_COLS[:i]) for i in range(PEER_TOPK + 2))
PAIR_ROWS = -(-PAIR_OFFS[-1] // SUBLANES) * SUBLANES


def _pack_halves(x):
    bits = pltpu.bitcast(x.astype(BF16).astype(F32), jnp.uint32)
    half = x.shape[0] // 2
    return (bits[half:] & jnp.uint32(0xFFFF0000)) | (bits[:half] >> 16)


def _peer_router_body(q_ref, sk_ref, nb_ref, e1_ref, rb_ref, e2_ref, a_ref, b_ref, g_ref):
    tt = q_ref.shape[0]
    for h in range(PEER_HEADS):
        s1 = _dot_nt(sk_ref[2 * h], q_ref[:, 2 * h * LANES:(2 * h + 1) * LANES])
        s2 = _dot_nt(sk_ref[2 * h + 1], q_ref[:, (2 * h + 1) * LANES:(2 * h + 2) * LANES])
        rank_a = _ranked_top(s1, PEER_TOPK + 1, a_ref)
        rank_b = _ranked_top(s2, PEER_TOPK + 1, b_ref)
        g_ref[PAIR_ROWS - SUBLANES:PAIR_ROWS, :] = jnp.full((SUBLANES, tt), NEG, F32)
        for i, n in enumerate(PAIR_COLS):
            g_ref[PAIR_OFFS[i]:PAIR_OFFS[i] + n, :] = a_ref[i:i + 1, :] + b_ref[0:n, :]
        pair_sums = g_ref[...]
        kth, nxt = _kth_and_next(pair_sums, PEER_TOPK)
        thr = 0.5 * (kth + nxt)
        a_top, b_top = a_ref[0:1, :], b_ref[0:1, :]
        chosen = pair_sums >= thr
        zsum = jnp.sum(jnp.where(chosen, jnp.exp(pair_sums - (a_top + b_top)), 0.0), axis=0, keepdims=True)
        ones = jnp.where(chosen, 1.0, 0.0)
        count_a = jnp.zeros_like(s1)
        for i in range(PEER_TOPK):
            n_i = jnp.sum(ones[PAIR_OFFS[i]:PAIR_OFFS[i + 1]], axis=0, keepdims=True)
            count_a = jnp.where(rank_a == float(i), n_i, count_a)
        e1 = jnp.exp(s1 - a_top).astype(BF16).astype(F32)
        for q in range(tt // LANES):
            nb_ref[q, h] = count_a[:, q * LANES:(q + 1) * LANES]
            e1_ref[q, h] = e1[:, q * LANES:(q + 1) * LANES]
        rb_ref[h] = _pack_halves(rank_b)
        e2_ref[h] = _pack_halves(jnp.exp(s2 - b_top) * (0.5 / zsum))


def _peer_router(q, sk, tt=256):
    t = q.shape[0]
    row_spec = pl.BlockSpec((tt // LANES, PEER_HEADS, PEER_KEYS, LANES), lambda i: (i, 0, 0, 0))
    row_shape = jax.ShapeDtypeStruct((t // LANES, PEER_HEADS, PEER_KEYS, LANES), F32)
    key_spec = pl.BlockSpec((PEER_HEADS, PEER_KEYS // 2, tt), lambda i: (0, 0, i))
    key_shape = jax.ShapeDtypeStruct((PEER_HEADS, PEER_KEYS // 2, t), jnp.uint32)
    return pl.pallas_call(
        _peer_router_body,
        grid=(t // tt,),
        in_specs=[pl.BlockSpec((tt, q.shape[1]), lambda i: (i, 0)),
                  pl.BlockSpec(sk.shape, lambda i: (0, 0, 0))],
        out_specs=[row_spec, row_spec, key_spec, key_spec],
        out_shape=[row_shape, row_shape, key_shape, key_shape],
        scratch_shapes=[pltpu.VMEM((24, tt), F32), pltpu.VMEM((24, tt), F32),
                        pltpu.VMEM((PAIR_ROWS, tt), F32)],
        compiler_params=_cparams(("parallel",)),
        name="peer_router",
    )(q, sk)


PEER_MROWS = 256


def _peer_dense_body(xt_ref, u_ref, vt_ref, nb_ref, e1_ref, rb_ref, e2_ref, o_ref, act_ref, wa_ref, *, rows):
    @pl.when(pl.program_id(1) == 0)
    def _():
        o_ref[...] = jnp.zeros_like(o_ref)

    et, tt = act_ref.shape
    half = PEER_KEYS // 2
    for m in range(et // PEER_MROWS):
        ms = slice(m * PEER_MROWS, (m + 1) * PEER_MROWS)
        act_ref[ms, :] = _dot(u_ref[ms, :], xt_ref[...])

    def gelu2(z):
        return z + z * lax.erf(z * (2.0 ** -0.5))

    for a in range(rows):
        for q in range(tt // LANES):
            tok = slice(q * LANES, (q + 1) * LANES)

            def row(ref, h):
                bits = pltpu.bitcast(jnp.broadcast_to(ref[q, h, a:a + 1, :], (SUBLANES, LANES)), jnp.uint32)
                return pltpu.bitcast((bits & jnp.uint32(0xFFFF0000)) | (bits >> 16), BF16)

            nbs = [row(nb_ref, h) for h in range(PEER_HEADS)]
            e1s = [row(e1_ref, h) for h in range(PEER_HEADS)]
            for s in range(half // SUBLANES):
                ks = slice(s * SUBLANES, (s + 1) * SUBLANES)
                w = None
                for h in range(PEER_HEADS):
                    rb = pltpu.bitcast(rb_ref[h, ks, tok], BF16)
                    e2 = pltpu.bitcast(e2_ref[h, ks, tok], BF16)
                    t = jnp.where(rb < nbs[h], e2 * e1s[h], 0.0)
                    w = t if w is None else w + t
                wbits = pltpu.bitcast(w, jnp.uint32)
                lo = slice(a * PEER_KEYS + s * SUBLANES, a * PEER_KEYS + (s + 1) * SUBLANES)
                hi = slice(a * PEER_KEYS + half + s * SUBLANES, a * PEER_KEYS + half + (s + 1) * SUBLANES)
                wa_ref[lo, tok] = (pltpu.bitcast(wbits << 16, F32) * gelu2(act_ref[lo, tok])).astype(BF16)
                wa_ref[hi, tok] = (pltpu.bitcast(wbits & jnp.uint32(0xFFFF0000), F32)
                                   * gelu2(act_ref[hi, tok])).astype(BF16)
    for m in range(o_ref.shape[0] // PEER_MROWS):
        ms = slice(m * PEER_MROWS, (m + 1) * PEER_MROWS)
        o_ref[ms, :] += _dot(vt_ref[ms, :], wa_ref[...])


def _peer_dense(xt, u, vt, nb, e1, rb, e2, tt=512, rows=8):
    d, t = xt.shape
    et = rows * PEER_KEYS
    tok_blk = pl.BlockSpec((d, tt), lambda i, j: (0, i))
    row_blk = pl.BlockSpec((tt // LANES, PEER_HEADS, rows, LANES), lambda i, j: (i, 0, j, 0))
    key_blk = pl.BlockSpec((PEER_HEADS, PEER_KEYS // 2, tt), lambda i, j: (0, 0, i))
    return pl.pallas_call(
        functools.partial(_peer_dense_body, rows=rows),
        grid=(t // tt, PEER_EXPERTS // et),
        in_specs=[tok_blk, pl.BlockSpec((et, d), lambda i, j: (j, 0)), pl.BlockSpec((d, et), lambda i, j: (0, j)),
                  row_blk, row_blk, key_blk, key_blk],
        out_specs=tok_blk,
        out_shape=jax.ShapeDtypeStruct((d, t), F32),
        scratch_shapes=[pltpu.VMEM((et, tt), F32), pltpu.VMEM((et, tt), BF16)],
        compiler_params=_cparams(("parallel", "arbitrary")),
        name="peer_dense",
    )(xt, u, vt, nb, e1, rb, e2)


def _ple_body(h_ref, yt_ref, g_ref, wg_ref, p_ref, wp_ref, fg_ref, o_ref, *, final):
    h = h_ref[...] + yt_ref[...].T
    gate = jax.nn.sigmoid(_dot(_rms(h, g_ref[...]).astype(BF16), wg_ref[...]))
    out = h + gate * _dot(p_ref[...].astype(BF16), wp_ref[...])
    o_ref[...] = _rms(out, fg_ref[...]) if final else out


def _ple(h, yt, g, wg, p, wp, fg, final, tm=512):
    t, d = h.shape
    full = lambda a: pl.BlockSpec(a.shape, lambda i: (0, 0))
    return pl.pallas_call(
        functools.partial(_ple_body, final=final),
        grid=(t // tm,),
        in_specs=[pl.BlockSpec((tm, d), lambda i: (i, 0)), pl.BlockSpec((d, tm), lambda i: (0, i)), full(g), full(wg),
                  pl.BlockSpec((tm, p.shape[1]), lambda i: (i, 0)), full(wp), full(fg)],
        out_specs=pl.BlockSpec((tm, d), lambda i: (i, 0)),
        out_shape=jax.ShapeDtypeStruct((t, d), F32),
        compiler_params=_cparams(("parallel",)),
        name="ple",
    )(h, yt, g, wg, p, wp, fg)


def _cast_transpose_body(x_ref, o_ref):
    o_ref[...] = x_ref[...].T.astype(o_ref.dtype)


def _cast_transpose(x, dtype, tile=1024):
    nl, r, c = x.shape
    return pl.pallas_call(
        _cast_transpose_body,
        grid=(nl, r // tile, c // tile),
        in_specs=[pl.BlockSpec((None, tile, tile), lambda l, i, j: (l, i, j))],
        out_specs=pl.BlockSpec((None, tile, tile), lambda l, i, j: (l, j, i)),
        out_shape=jax.ShapeDtypeStruct((nl, c, r), dtype),
        compiler_params=_cparams(("parallel", "parallel", "parallel")),
        name="cast_transpose",
    )(x)


def _rot_half_cols(w):
    half = w.shape[-1] // 2
    return jnp.concatenate([-w[..., half:], w[..., :half]], axis=-1)


def _pad_cols(w, width):
    return jnp.pad(w, [(0, 0)] * (w.ndim - 1) + [(0, width - w.shape[-1])])


def _prep_w_in(w_in):
    offs = [0]
    for s in IN_SIZES:
        offs.append(offs[-1] + s)
    (d_q, d_k, d_v, g_q, g_f, g_v, g_g, m_cq, m_ckv, m_kr,
     l_q, l_k, l_v, l_i, l_f, l_o) = (w_in[..., offs[i]:offs[i + 1]] for i in range(len(IN_SIZES)))
    pieces = [d_q, d_k, d_v, g_q, g_f, g_v, g_g, l_v, l_o, m_ckv, l_q, l_k, m_cq,
              _pad_cols(m_kr, LANES), _pad_cols(_rot_half_cols(m_kr), LANES),
              _pad_cols(jnp.concatenate([l_i, l_f], axis=-1), LANES)]
    return jnp.concatenate(pieces, axis=-1)


def _prep_mla(w_uq, w_ukv):
    nl = w_uq.shape[0]
    wq = w_uq.reshape(nl, MLA_Q_RANK, HEADS, MLA_NOPE + MLA_ROPE)
    rope = wq[..., MLA_NOPE:]
    wq_all = jnp.concatenate([
        wq[..., :MLA_NOPE].reshape(nl, MLA_Q_RANK, MIX_W),
        _pad_cols(rope, LANES).reshape(nl, MLA_Q_RANK, MIX_W),
        _pad_cols(_rot_half_cols(rope), LANES).reshape(nl, MLA_Q_RANK, MIX_W)], axis=-1)
    wkv = w_ukv.reshape(nl, MLA_KV_RANK, HEADS, MLA_NOPE + HEAD_V)
    wkv_all = jnp.concatenate([wkv[..., :MLA_NOPE].reshape(nl, MLA_KV_RANK, MIX_W),
                               wkv[..., MLA_NOPE:].reshape(nl, MLA_KV_RANK, MIX_W)], axis=-1)
    return wq_all.astype(BF16), wkv_all.astype(BF16)


def kernel(x, p, positions, ln_mix, w_in, diff_lambda, diff_norm, hgrn_lb_logits, hgrn_norm, mla_q_norm,
           mla_kv_norm, mla_w_uq, mla_w_ukv, mlstm_conv_w, mlstm_conv_b, mlstm_gate_b, mlstm_norm, group_gain,
           w_out, ln_ffn, peer_w_q, peer_subkeys, peer_u, peer_v, ln_ple, ple_w_gate, ple_w_proj, ln_final):
    batch, seq, d = x.shape
    depth = w_in.shape[0]
    t = batch * seq

    w_in_p = _prep_w_in(w_in.astype(BF16))
    wq_all, wkv_all = _prep_mla(mla_w_uq, mla_w_ukv)
    w_out_b, peer_wq_b, ple_wg_b, ple_wp_b = (a.astype(BF16) for a in (w_out, peer_w_q, ple_w_gate, ple_w_proj))
    peer_u_b, peer_vt_b = peer_u.astype(BF16), _cast_transpose(peer_v, BF16)
    subkeys_b = peer_subkeys.reshape(depth, PEER_HEADS * 2, PEER_KEYS, PEER_KEY_DIM).astype(BF16)

    lv = diff_lambda.astype(F32)
    lam = jnp.exp(jnp.sum(lv[:, 0] * lv[:, 1], axis=-1)) - jnp.exp(jnp.sum(lv[:, 2] * lv[:, 3], axis=-1))
    lbs = jnp.cumsum(jax.nn.softmax(hgrn_lb_logits.astype(F32), axis=0), axis=0)
    lbs = lbs - lbs[:1]
    hgrn_par = jnp.stack([jnp.maximum(jnp.log(lbs), NEG), jnp.log1p(-lbs), 1.0 - lbs, hgrn_norm], axis=1)
    hgrn_par = jnp.pad(hgrn_par, ((0, 0), (0, 4), (0, 0)))
    gate_b = _pad_cols(mlstm_gate_b.reshape(depth, 1, 2 * HEADS), LANES)
    half = MLA_ROPE // 2
    inv = ROPE_THETA ** (-jnp.arange(half, dtype=F32) / half)
    inv = _pad_cols(jnp.concatenate([inv, inv])[None, :], LANES)
    pos = positions.astype(F32).reshape(t, 1)
    row = lambda a: a.reshape(1, -1)

    h = x.reshape(t, d)
    for i in range(depth):
        lam_init = 0.8 - 0.6 * math.exp(-0.3 * i)
        z = _norm_mm(h, row(ln_mix[i]), w_in_p[i], F32, tm=512, tn=2048)
        y_diff = _diff_attn(z, (lam[i] + lam_init).reshape(1, 1), row(diff_norm[i]), batch, seq, lam_init)
        y_hgrn = _hgrn(z, hgrn_par[i], batch, seq)
        mq, mk, mv = _mla_proj(z, pos, inv, row(mla_q_norm[i]), row(mla_kv_norm[i]), wq_all[i], wkv_all[i])
        y_mla = _mla_attn(mq, mk, mv, batch, seq)
        qk = _conv_silu(z, mlstm_conv_w[i], row(mlstm_conv_b[i]), batch, seq)
        y_mlstm = _mlstm(qk, z, gate_b[i], row(mlstm_norm[i]), batch, seq)
        h = _out_proj(h, (y_diff, y_hgrn, y_mla, y_mlstm), row(group_gain[i]), w_out_b[i])
        pq, xn = _norm_mm(h, row(ln_ffn[i]), peer_wq_b[i], BF16, tm=512, tn=2048, emit_xn=True)
        yt = _peer_dense(xn.T, peer_u_b[i], peer_vt_b[i], *_peer_router(pq, subkeys_b[i]))
        h = _ple(h, yt, row(ln_ple[i]), ple_wg_b[i], p[i].reshape(t, PLE_DIM), ple_wp_b[i], row(ln_final),
                 final=(i == depth - 1))
    return h.reshape(batch, seq, d)
```

```python
import functools
import math

import jax
import jax.numpy as jnp
from jax import lax
from jax.experimental import pallas as pl
from jax.experimental.pallas import tpu as pltpu

F32 = jnp.float32
BF16 = jnp.bfloat16

D_MODEL = 2048
PLE_DIM = 256
RMS_EPS = 1e-6
ROPE_THETA = 10000.0
HEADS = 4
HEAD_V = 128
MIX_W = HEADS * HEAD_V
DIFF_QK = 64
HGRN_DK = 128
MLA_NOPE = 128
MLA_ROPE = 64
MLA_Q_RANK = 384
MLA_KV_RANK = 256
MLA_QK_PAD = 256
MLSTM_DQK = 64
MLSTM_CONV = 4
CHUNK = 128
HGRN_CHUNK = 64
SUB = 16
PEER_HEADS = 8
PEER_KEYS = 128
PEER_KEY_DIM = 128
PEER_TOPK = 16
PEER_EXPERTS = PEER_KEYS * PEER_KEYS
NEG = -1e30

VMEM_LIMIT = 56 * 1024 * 1024
LANES = 128
SUBLANES = 8

IN_SIZES = (512, 512, 512, 512, 512, 512, 512, 384, 256, 64, 256, 256, 512, 4, 4, 512)
C_DQ, C_DK, C_DV, C_GQ, C_GF, C_GV, C_GG, C_LV, C_LO = (i * 512 for i in range(9))
C_CKV, C_LQ, C_LK = 4608, 4864, 5120
C_CQ = 5376
C_KR, C_KRR, C_GATE = 5760, 5888, 6016
N_IN = 6144


def _cparams(sem):
    return pltpu.CompilerParams(dimension_semantics=sem, vmem_limit_bytes=VMEM_LIMIT)


def _rms(x, g):
    return x * lax.rsqrt(jnp.mean(x * x, axis=-1, keepdims=True) + RMS_EPS) * g


def _dot(a, b):
    return jnp.dot(a, b, preferred_element_type=F32)


def _dot_nt(a, b):
    return lax.dot_general(a, b, (((1,), (1,)), ((), ())), preferred_element_type=F32)


def _dot_tn(a, b):
    return lax.dot_general(a, b, (((0,), (0,)), ((), ())), preferred_element_type=F32)


def _log_sigmoid(x):
    return jnp.minimum(x, 0.0) - jnp.log1p(jnp.exp(-jnp.abs(x)))


def _cumsum_rows(x):
    row = lax.broadcasted_iota(jnp.int32, x.shape, 0)
    k = 1
    while k < x.shape[0]:
        x = x + jnp.where(row >= k, pltpu.roll(x, k, 0), 0.0)
        k *= 2
    return x


def _norm_mm_body(x_ref, g_ref, w_ref, o_ref, *rest, emit_xn):
    xn_ref = rest[-1]

    @pl.when(pl.program_id(1) == 0)
    def _():
        y = _rms(x_ref[...], g_ref[...]).astype(BF16)
        xn_ref[...] = y
        if emit_xn:
            rest[0][...] = y

    o_ref[...] = _dot(xn_ref[...], w_ref[...]).astype(o_ref.dtype)


def _norm_mm(x, g, w, out_dtype, tm, tn, emit_xn=False):
    t, d = x.shape
    n = w.shape[1]
    out_shape = [jax.ShapeDtypeStruct((t, n), out_dtype)]
    out_specs = [pl.BlockSpec((tm, tn), lambda i, j: (i, j))]
    if emit_xn:
        out_shape.append(jax.ShapeDtypeStruct((t, d), BF16))
        out_specs.append(pl.BlockSpec((tm, d), lambda i, j: (i, 0)))
    res = pl.pallas_call(
        functools.partial(_norm_mm_body, emit_xn=emit_xn),
        grid=(t // tm, n // tn),
        in_specs=[pl.BlockSpec((tm, d), lambda i, j: (i, 0)),
                  pl.BlockSpec((1, d), lambda i, j: (0, 0)),
                  pl.BlockSpec((d, tn), lambda i, j: (0, j))],
        out_specs=out_specs,
        out_shape=out_shape,
        scratch_shapes=[pltpu.VMEM((tm, d), BF16)],
        compiler_params=_cparams(("parallel", "arbitrary")),
        name="norm_mm",
    )(x, g, w)
    return res if emit_xn else res[0]


def _causal_mask(i, tq):
    row = i * tq + lax.broadcasted_iota(jnp.int32, (tq, (i + 1) * tq), 0)
    col = lax.broadcasted_iota(jnp.int32, (tq, (i + 1) * tq), 1)
    return col <= row


def _softmax_terms(s, mask):
    s = jnp.where(mask, s, NEG)
    e = jnp.exp(s - jnp.max(s, axis=-1, keepdims=True))
    return e, 1.0 / jnp.sum(e, axis=-1, keepdims=True)


def _per_query_block(nq, fn):
    for i in range(nq):
        pl.when(pl.program_id(2) == i)(functools.partial(fn, i))


def _diff_attn_body(lam_ref, q_ref, k_ref, v_ref, g_ref, o_ref, *, tq, nq, lam_init):
    def block(i):
        kv = (i + 1) * tq
        mask = _causal_mask(i, tq)
        q = q_ref[...] * (DIFF_QK ** -0.5)
        lane = lax.broadcasted_iota(jnp.int32, q.shape, 1)
        k = k_ref[0:kv, :].astype(BF16)
        e1, r1 = _softmax_terms(_dot_nt(jnp.where(lane < DIFF_QK, q, 0.0).astype(BF16), k), mask)
        e2, r2 = _softmax_terms(_dot_nt(jnp.where(lane >= DIFF_QK, q, 0.0).astype(BF16), k), mask)
        a = e1 * r1 - e2 * (lam_ref[0, 0] * r2)
        o = _dot(a.astype(BF16), v_ref[0:kv, :].astype(BF16))
        o_ref[...] = _rms(o, g_ref[...]) * (1.0 - lam_init)

    _per_query_block(nq, block)


def _diff_attn(z, lam, g, batch, seq, lam_init, tq=256):
    nq = seq // tq
    return pl.pallas_call(
        functools.partial(_diff_attn_body, tq=tq, nq=nq, lam_init=lam_init),
        grid=(batch, HEADS, nq),
        in_specs=[pl.BlockSpec(memory_space=pltpu.SMEM),
                  pl.BlockSpec((tq, LANES), lambda b, h, i: (b * nq + i, C_DQ // LANES + h)),
                  pl.BlockSpec((seq, LANES), lambda b, h, i: (b, C_DK // LANES + h)),
                  pl.BlockSpec((seq, LANES), lambda b, h, i: (b, C_DV // LANES + h)),
                  pl.BlockSpec((1, LANES), lambda b, h, i: (0, h))],
        out_specs=pl.BlockSpec((tq, LANES), lambda b, h, i: (b * nq + i, h)),
        out_shape=jax.ShapeDtypeStruct((batch * seq, MIX_W), F32),
        compiler_params=_cparams(("parallel", "parallel", "arbitrary")),
        name="diff_attn",
    )(lam, z, z, z, g)


def _mla_proj_body(cq_ref, ckv_ref, kr_ref, krr_ref, pos_ref, inv_ref, qn_ref, kvn_ref,
                   wq_ref, wkv_ref, q_ref, k_ref, v_ref):
    qa = _dot(_rms(cq_ref[...], qn_ref[...]).astype(BF16), wq_ref[...])
    kva = _dot(_rms(ckv_ref[...], kvn_ref[...]).astype(BF16), wkv_ref[...])
    ang = pos_ref[...] * inv_ref[...]
    cos, sin = jnp.cos(ang), jnp.sin(ang)
    k_rope = (kr_ref[...] * cos + krr_ref[...] * sin).astype(BF16)
    scale = (MLA_NOPE + MLA_ROPE) ** -0.5
    for h in range(HEADS):
        lo, hi = h * LANES, (h + 1) * LANES
        q_rope = qa[:, MIX_W + lo:MIX_W + hi] * cos + qa[:, 2 * MIX_W + lo:2 * MIX_W + hi] * sin
        q_ref[:, 2 * lo:2 * lo + LANES] = (qa[:, lo:hi] * scale).astype(BF16)
        q_ref[:, 2 * lo + LANES:2 * hi] = (q_rope * scale).astype(BF16)
        k_ref[:, 2 * lo:2 * lo + LANES] = kva[:, lo:hi].astype(BF16)
        k_ref[:, 2 * lo + LANES:2 * hi] = k_rope
    v_ref[...] = kva[:, MIX_W:].astype(BF16)


def _mla_proj(z, pos, inv, qn, kvn, wq, wkv, tm=512):
    t = z.shape[0]
    row = lambda c, w: pl.BlockSpec((tm, w), lambda i: (i, c // w))
    full = lambda a: pl.BlockSpec(a.shape, lambda i: (0, 0))
    return pl.pallas_call(
        _mla_proj_body,
        grid=(t // tm,),
        in_specs=[row(C_CQ, MLA_Q_RANK), row(C_CKV, MLA_KV_RANK), row(C_KR, LANES), row(C_KRR, LANES),
                  pl.BlockSpec((tm, 1), lambda i: (i, 0)), full(inv), full(qn), full(kvn), full(wq), full(wkv)],
        out_specs=[pl.BlockSpec((tm, HEADS * MLA_QK_PAD), lambda i: (i, 0)),
                   pl.BlockSpec((tm, HEADS * MLA_QK_PAD), lambda i: (i, 0)),
                   pl.BlockSpec((tm, MIX_W), lambda i: (i, 0))],
        out_shape=[jax.ShapeDtypeStruct((t, HEADS * MLA_QK_PAD), BF16),
                   jax.ShapeDtypeStruct((t, HEADS * MLA_QK_PAD), BF16),
                   jax.ShapeDtypeStruct((t, MIX_W), BF16)],
        compiler_params=_cparams(("parallel",)),
        name="mla_proj",
    )(z, z, z, z, pos, inv, qn, kvn, wq, wkv)


def _mla_attn_body(q_ref, k_ref, v_ref, o_ref, *, tq, nq):
    def block(i):
        kv = (i + 1) * tq
        e, r = _softmax_terms(_dot_nt(q_ref[...], k_ref[0:kv, :]), _causal_mask(i, tq))
        o_ref[...] = _dot((e * r).astype(BF16), v_ref[0:kv, :])

    _per_query_block(nq, block)


def _mla_attn(q, k, v, batch, seq, tq=256):
    nq = seq // tq
    return pl.pallas_call(
        functools.partial(_mla_attn_body, tq=tq, nq=nq),
        grid=(batch, HEADS, nq),
        in_specs=[pl.BlockSpec((tq, MLA_QK_PAD), lambda b, h, i: (b * nq + i, h)),
                  pl.BlockSpec((seq, MLA_QK_PAD), lambda b, h, i: (b, h)),
                  pl.BlockSpec((seq, LANES), lambda b, h, i: (b, h))],
        out_specs=pl.BlockSpec((tq, LANES), lambda b, h, i: (b * nq + i, h)),
        out_shape=jax.ShapeDtypeStruct((batch * seq, MIX_W), F32),
        compiler_params=_cparams(("parallel", "parallel", "arbitrary")),
        name="mla_attn",
    )(q, k, v)


def _hgrn_body(q_ref, f_ref, v_ref, g_ref, par_ref, o_ref, st_ref):
    @pl.when(pl.program_id(1) == 0)
    def _():
        st_ref[...] = jnp.zeros_like(st_ref)

    sub_row = lax.broadcasted_iota(jnp.int32, (SUB, SUB), 0)
    sub_col = lax.broadcasted_iota(jnp.int32, (SUB, SUB), 1)
    for h in range(HEADS):
        sl = slice(h * LANES, (h + 1) * LANES)
        log_lb, log_1m_lb, one_m_lb, gain = (par_ref[r:r + 1, sl] for r in range(4))
        zf = f_ref[:, sl]
        q = q_ref[:, sl] * (HGRN_DK ** -0.5)
        v = v_ref[:, sl].astype(BF16)
        k = one_m_lb * jax.nn.sigmoid(-zf)
        lf_a, lf_b = log_lb, log_1m_lb + _log_sigmoid(zf)
        log_f = jnp.maximum(lf_a, lf_b) + jnp.log1p(jnp.exp(-jnp.abs(lf_a - lf_b)))
        b = _cumsum_rows(log_f)
        st = st_ref[h]
        o = _dot_nt((q * jnp.exp(b)).astype(BF16), st.astype(BF16))
        parts = []
        for i in range(HGRN_CHUNK // SUB):
            lo = i * SUB
            qi, bi, ki, vi = q[lo:lo + SUB], b[lo:lo + SUB], k[lo:lo + SUB], v[lo:lo + SUB]
            sc = jnp.zeros((SUB, SUB), F32)
            for s in range(SUB):
                dec = jnp.exp(jnp.minimum(bi - bi[s:s + 1], 0.0))
                col = jnp.sum(qi * ki[s:s + 1] * dec, axis=-1, keepdims=True)
                sc = sc + jnp.where(sub_col == s, col, 0.0)
            sc = jnp.where(sub_col <= sub_row, sc, 0.0)
            part = _dot(sc.astype(BF16), vi)
            if i > 0:
                b0 = b[lo - 1:lo]
                qe = (qi * jnp.exp(bi - b0)).astype(BF16)
                ke = (k[:lo] * jnp.exp(b0 - b[:lo])).astype(BF16)
                part = part + _dot(_dot_nt(qe, ke).astype(BF16), v[:lo])
            parts.append(part)
        o = o + jnp.concatenate(parts, axis=0)
        b_last = b[HGRN_CHUNK - 1:HGRN_CHUNK]
        st_ref[h] = st * jnp.exp(b_last) + _dot_tn(v, (k * jnp.exp(b_last - b)).astype(BF16))
        o_ref[:, sl] = _rms(o, gain) * jax.nn.silu(g_ref[:, sl])


def _hgrn(z, par, batch, seq):
    nc = seq // HGRN_CHUNK
    blk = lambda c: pl.BlockSpec((HGRN_CHUNK, MIX_W), lambda b, i: (b * nc + i, c // MIX_W))
    return pl.pallas_call(
        _hgrn_body,
        grid=(batch, nc),
        in_specs=[blk(C_GQ), blk(C_GF), blk(C_GV), blk(C_GG), pl.BlockSpec((8, MIX_W), lambda b, i: (0, 0))],
        out_specs=pl.BlockSpec((HGRN_CHUNK, MIX_W), lambda b, i: (b * nc + i, 0)),
        out_shape=jax.ShapeDtypeStruct((batch * seq, MIX_W), F32),
        scratch_shapes=[pltpu.VMEM((HEADS, HEAD_V, HGRN_DK), F32)],
        compiler_params=_cparams(("parallel", "arbitrary")),
        name="hgrn2",
    )(z, z, z, z, par)


def _conv_silu_body(x_ref, w_ref, b_ref, o_ref):
    x = x_ref[...]
    row = lax.broadcasted_iota(jnp.int32, x.shape, 0)
    y = x * w_ref[MLSTM_CONV - 1:MLSTM_CONV] + b_ref[...]
    for k in range(1, MLSTM_CONV):
        y = y + jnp.where(row >= k, pltpu.roll(x, k, 0), 0.0) * w_ref[MLSTM_CONV - 1 - k:MLSTM_CONV - k]
    o_ref[...] = jax.nn.silu(y)


def _conv_silu(z, w, b, batch, seq):
    wd = HEADS * MLSTM_DQK
    return pl.pallas_call(
        _conv_silu_body,
        grid=(batch, 2),
        in_specs=[pl.BlockSpec((seq, wd), lambda bi, j: (bi, C_LQ // wd + j)),
                  pl.BlockSpec((MLSTM_CONV, wd), lambda bi, j: (0, j)),
                  pl.BlockSpec((1, wd), lambda bi, j: (0, j))],
        out_specs=pl.BlockSpec((seq, wd), lambda bi, j: (bi, j)),
        out_shape=jax.ShapeDtypeStruct((batch * seq, 2 * wd), F32),
        compiler_params=_cparams(("parallel", "parallel")),
        name="conv_silu",
    )(z, w, b)


def _mlstm_body(qk_ref, v_ref, gate_ref, og_ref, gb_ref, norm_ref, o_ref, c_ref, m_ref):
    @pl.when(pl.program_id(1) == 0)
    def _():
        c_ref[...] = jnp.zeros_like(c_ref)
        m_ref[...] = jnp.zeros_like(m_ref)

    g = gate_ref[...] + gb_ref[...]
    lane = lax.broadcasted_iota(jnp.int32, g.shape, 1)
    gt = jnp.where(lane < HEADS, g, _cumsum_rows(_log_sigmoid(g)))
    gt_t = gt.T
    row = lax.broadcasted_iota(jnp.int32, (CHUNK, CHUNK), 0)
    col = lax.broadcasted_iota(jnp.int32, (CHUNK, CHUNK), 1)
    ones_col = jnp.where(lane == 0, 1.0, 0.0).astype(BF16)
    half_row = lax.broadcasted_iota(jnp.int32, (LANES, 1), 0) < MLSTM_DQK
    for p in range(HEADS // 2):
        q_t = qk_ref[:, p * LANES:(p + 1) * LANES] * (MLSTM_DQK ** -0.5)
        k_t = qk_ref[:, HEADS * MLSTM_DQK + p * LANES:HEADS * MLSTM_DQK + (p + 1) * LANES]
        c_old = c_ref[p]
        c_bf = c_old.astype(BF16)
        upd = jnp.zeros_like(c_old)
        carries = []
        for u in range(2):
            h = 2 * p + u
            mine = (lane >= u * MLSTM_DQK) & (lane < (u + 1) * MLSTM_DQK)
            li_col, b_col = gt[:, h:h + 1], gt[:, HEADS + h:HEADS + h + 1]
            li_row, b_row = gt_t[h:h + 1, :], gt_t[HEADS + h:HEADS + h + 1, :]
            m_st = m_ref[h:h + 1, 0:1]
            dmat = jnp.where(col <= row, b_col - b_row + li_row, NEG)
            inter_log = b_col + m_st
            m_t = jnp.maximum(inter_log, jnp.max(dmat, axis=-1, keepdims=True))
            qm = jnp.where(mine, q_t, 0.0).astype(BF16)
            s = _dot_nt(qm, k_t.astype(BF16)) * jnp.exp(dmat - m_t)
            v_aug = jnp.concatenate([v_ref[:, h * LANES:(h + 1) * LANES].astype(BF16), ones_col], axis=1)
            out = jnp.exp(inter_log - m_t) * _dot(qm, c_bf) + _dot(s.astype(BF16), v_aug)
            hid = out[:, :HEAD_V] / jnp.maximum(jnp.abs(out[:, HEAD_V:HEAD_V + 1]), jnp.exp(-m_t))
            sl = slice(h * LANES, (h + 1) * LANES)
            o_ref[:, sl] = _rms(hid, norm_ref[:, sl]) * jax.nn.sigmoid(og_ref[:, sl])
            b_last = b_col[CHUNK - 1:CHUNK]
            log_s = b_last - b_col + li_col
            m_new = jnp.maximum(b_last + m_st, jnp.max(log_s, axis=0, keepdims=True))
            carries.append(jnp.exp(b_last + m_st - m_new))
            kw = (jnp.where(mine, k_t, 0.0) * jnp.exp(log_s - m_new)).astype(BF16)
            upd = upd + _dot_tn(kw, v_aug)
            m_ref[h:h + 1, :] = jnp.broadcast_to(m_new, (1, LANES))
        c_ref[p] = jnp.where(half_row, carries[0], carries[1]) * c_old + upd


def _mlstm(qk, z, gb, norm, batch, seq):
    nc = seq // CHUNK
    return pl.pallas_call(
        _mlstm_body,
        grid=(batch, nc),
        in_specs=[pl.BlockSpec((CHUNK, 2 * HEADS * MLSTM_DQK), lambda b, i: (b * nc + i, 0)),
                  pl.BlockSpec((CHUNK, MIX_W), lambda b, i: (b * nc + i, C_LV // MIX_W)),
                  pl.BlockSpec((CHUNK, LANES), lambda b, i: (b * nc + i, C_GATE // LANES)),
                  pl.BlockSpec((CHUNK, MIX_W), lambda b, i: (b * nc + i, C_LO // MIX_W)),
                  pl.BlockSpec((1, LANES), lambda b, i: (0, 0)),
                  pl.BlockSpec((1, MIX_W), lambda b, i: (0, 0))],
        out_specs=pl.BlockSpec((CHUNK, MIX_W), lambda b, i: (b * nc + i, 0)),
        out_shape=jax.ShapeDtypeStruct((batch * seq, MIX_W), F32),
        scratch_shapes=[pltpu.VMEM((HEADS // 2, LANES, 2 * HEAD_V), F32), pltpu.VMEM((8, LANES), F32)],
        compiler_params=_cparams(("parallel", "arbitrary")),
        name="mlstm",
    )(qk, z, z, z, gb, norm)


def _out_proj_body(h_ref, y0_ref, y1_ref, y2_ref, y3_ref, g_ref, w_ref, o_ref):
    acc = h_ref[...]
    for i, y_ref in enumerate((y0_ref, y1_ref, y2_ref, y3_ref)):
        sl = slice(i * MIX_W, (i + 1) * MIX_W)
        acc = acc + _dot((y_ref[...] * g_ref[:, sl]).astype(BF16), w_ref[sl, :])
    o_ref[...] = acc


def _out_proj(h, ys, g, w, tm=512):
    t, d = h.shape
    yspec = pl.BlockSpec((tm, MIX_W), lambda i: (i, 0))
    return pl.pallas_call(
        _out_proj_body,
        grid=(t // tm,),
        in_specs=[pl.BlockSpec((tm, d), lambda i: (i, 0)), yspec, yspec, yspec, yspec,
                  pl.BlockSpec((1, HEADS * MIX_W), lambda i: (0, 0)),
                  pl.BlockSpec((HEADS * MIX_W, d), lambda i: (0, 0))],
        out_specs=pl.BlockSpec((tm, d), lambda i: (i, 0)),
        out_shape=jax.ShapeDtypeStruct((t, d), F32),
        compiler_params=_cparams(("parallel",)),
        name="out_proj",
    )(h, *ys, g, w)


def _ranked_top(x, n, dst_ref):
    rank = jnp.full(x.shape, float(n), F32)
    for r in range(n):
        mx = jnp.max(x, axis=0, keepdims=True)
        dst_ref[r:r + 1, :] = mx
        eq = x == mx
        rank = jnp.where(eq, float(r), rank)
        x = jnp.where(eq, NEG, x)
    return rank


def _kth_and_next(x, k):
    cum = jnp.zeros((1, x.shape[1]), F32)
    kth = jnp.full((1, x.shape[1]), NEG, F32)
    nxt = kth
    for _ in range(k + 1):
        mx = jnp.max(x, axis=0, keepdims=True)
        eq = x == mx
        new = cum + jnp.sum(eq.astype(F32), axis=0, keepdims=True)
        kth = jnp.where((cum < k) & (new >= k), mx, kth)
        nxt = jnp.where((cum < k + 1) & (new >= k + 1), mx, nxt)
        cum = new
        x = jnp.where(eq, NEG, x)
    return kth, nxt


PAIR_COLS = tuple((PEER_TOPK + 1) // (i + 1) for i in range(PEER_TOPK + 1))
PAIR_OFFS = tuple(sum(PAIR_COLS[:i]) for i in range(PEER_TOPK + 2))
PAIR_ROWS = -(-PAIR_OFFS[-1] // SUBLANES) * SUBLANES


def _pack_halves(x):
    bits = pltpu.bitcast(x.astype(BF16).astype(F32), jnp.uint32)
    half = x.shape[0] // 2
    return (bits[half:] & jnp.uint32(0xFFFF0000)) | (bits[:half] >> 16)


def _peer_router_body(q_ref, sk_ref, nb_ref, e1_ref, rb_ref, e2_ref, a_ref, b_ref, g_ref):
    tt = q_ref.shape[0]
    for h in range(PEER_HEADS):
        s1 = _dot_nt(sk_ref[2 * h], q_ref[:, 2 * h * LANES:(2 * h + 1) * LANES])
        s2 = _dot_nt(sk_ref[2 * h + 1], q_ref[:, (2 * h + 1) * LANES:(2 * h + 2) * LANES])
        rank_a = _ranked_top(s1, PEER_TOPK + 1, a_ref)
        rank_b = _ranked_top(s2, PEER_TOPK + 1, b_ref)
        g_ref[PAIR_ROWS - SUBLANES:PAIR_ROWS, :] = jnp.full((SUBLANES, tt), NEG, F32)
        for i, n in enumerate(PAIR_COLS):
            g_ref[PAIR_OFFS[i]:PAIR_OFFS[i] + n, :] = a_ref[i:i + 1, :] + b_ref[0:n, :]
        pair_sums = g_ref[...]
        kth, nxt = _kth_and_next(pair_sums, PEER_TOPK)
        thr = 0.5 * (kth + nxt)
        a_top, b_top = a_ref[0:1, :], b_ref[0:1, :]
        chosen = pair_sums >= thr
        zsum = jnp.sum(jnp.where(chosen, jnp.exp(pair_sums - (a_top + b_top)), 0.0), axis=0, keepdims=True)
        ones = jnp.where(chosen, 1.0, 0.0)
        count_a = jnp.zeros_like(s1)
        for i in range(PEER_TOPK):
            n_i = jnp.sum(ones[PAIR_OFFS[i]:PAIR_OFFS[i + 1]], axis=0, keepdims=True)
            count_a = jnp.where(rank_a == float(i), n_i, count_a)
        e1 = jnp.exp(s1 - a_top).astype(BF16).astype(F32)
        for q in range(tt // LANES):
            nb_ref[q, h] = count_a[:, q * LANES:(q + 1) * LANES]
            e1_ref[q, h] = e1[:, q * LANES:(q + 1) * LANES]
        rb_ref[h] = _pack_halves(rank_b)
        e2_ref[h] = _pack_halves(jnp.exp(s2 - b_top) * (0.5 / zsum))


def _peer_router(q, sk, tt=256):
    t = q.shape[0]
    row_spec = pl.BlockSpec((tt // LANES, PEER_HEADS, PEER_KEYS, LANES), lambda i: (i, 0, 0, 0))
    row_shape = jax.ShapeDtypeStruct((t // LANES, PEER_HEADS, PEER_KEYS, LANES), F32)
    key_spec = pl.BlockSpec((PEER_HEADS, PEER_KEYS // 2, tt), lambda i: (0, 0, i))
    key_shape = jax.ShapeDtypeStruct((PEER_HEADS, PEER_KEYS // 2, t), jnp.uint32)
    return pl.pallas_call(
        _peer_router_body,
        grid=(t // tt,),
        in_specs=[pl.BlockSpec((tt, q.shape[1]), lambda i: (i, 0)),
                  pl.BlockSpec(sk.shape, lambda i: (0, 0, 0))],
        out_specs=[row_spec, row_spec, key_spec, key_spec],
        out_shape=[row_shape, row_shape, key_shape, key_shape],
        scratch_shapes=[pltpu.VMEM((24, tt), F32), pltpu.VMEM((24, tt), F32),
                        pltpu.VMEM((PAIR_ROWS, tt), F32)],
        compiler_params=_cparams(("parallel",)),
        name="peer_router",
    )(q, sk)


PEER_MROWS = 256


def _peer_dense_body(xt_ref, u_ref, vt_ref, nb_ref, e1_ref, rb_ref, e2_ref, o_ref, act_ref, wa_ref, *, rows):
    @pl.when(pl.program_id(1) == 0)
    def _():
        o_ref[...] = jnp.zeros_like(o_ref)

    et, tt = act_ref.shape
    half = PEER_KEYS // 2
    for m in range(et // PEER_MROWS):
        ms = slice(m * PEER_MROWS, (m + 1) * PEER_MROWS)
        act_ref[ms, :] = _dot(u_ref[ms, :], xt_ref[...])

    def gelu2(z):
        return z + z * lax.erf(z * (2.0 ** -0.5))

    for a in range(rows):
        for q in range(tt // LANES):
            tok = slice(q * LANES, (q + 1) * LANES)

            def row(ref, h):
                bits = pltpu.bitcast(jnp.broadcast_to(ref[q, h, a:a + 1, :], (SUBLANES, LANES)), jnp.uint32)
                return pltpu.bitcast((bits & jnp.uint32(0xFFFF0000)) | (bits >> 16), BF16)

            nbs = [row(nb_ref, h) for h in range(PEER_HEADS)]
            e1s = [row(e1_ref, h) for h in range(PEER_HEADS)]
            for s in range(half // SUBLANES):
                ks = slice(s * SUBLANES, (s + 1) * SUBLANES)
                w = None
                for h in range(PEER_HEADS):
                    rb = pltpu.bitcast(rb_ref[h, ks, tok], BF16)
                    e2 = pltpu.bitcast(e2_ref[h, ks, tok], BF16)
                    t = jnp.where(rb < nbs[h], e2 * e1s[h], 0.0)
                    w = t if w is None else w + t
                wbits = pltpu.bitcast(w, jnp.uint32)
                lo = slice(a * PEER_KEYS + s * SUBLANES, a * PEER_KEYS + (s + 1) * SUBLANES)
                hi = slice(a * PEER_KEYS + half + s * SUBLANES, a * PEER_KEYS + half + (s + 1) * SUBLANES)
                wa_ref[lo, tok] = (pltpu.bitcast(wbits << 16, F32) * gelu2(act_ref[lo, tok])).astype(BF16)
                wa_ref[hi, tok] = (pltpu.bitcast(wbits & jnp.uint32(0xFFFF0000), F32)
                                   * gelu2(act_ref[hi, tok])).astype(BF16)
    for m in range(o_ref.shape[0] // PEER_MROWS):
        ms = slice(m * PEER_MROWS, (m + 1) * PEER_MROWS)
        o_ref[ms, :] += _dot(vt_ref[ms, :], wa_ref[...])


def _peer_dense(xt, u, vt, nb, e1, rb, e2, tt=512, rows=8):
    d, t = xt.shape
    et = rows * PEER_KEYS
    tok_blk = pl.BlockSpec((d, tt), lambda i, j: (0, i))
    row_blk = pl.BlockSpec((tt // LANES, PEER_HEADS, rows, LANES), lambda i, j: (i, 0, j, 0))
    key_blk = pl.BlockSpec((PEER_HEADS, PEER_KEYS // 2, tt), lambda i, j: (0, 0, i))
    return pl.pallas_call(
        functools.partial(_peer_dense_body, rows=rows),
        grid=(t // tt, PEER_EXPERTS // et),
        in_specs=[tok_blk, pl.BlockSpec((et, d), lambda i, j: (j, 0)), pl.BlockSpec((d, et), lambda i, j: (0, j)),
                  row_blk, row_blk, key_blk, key_blk],
        out_specs=tok_blk,
        out_shape=jax.ShapeDtypeStruct((d, t), F32),
        scratch_shapes=[pltpu.VMEM((et, tt), F32), pltpu.VMEM((et, tt), BF16)],
        compiler_params=_cparams(("parallel", "arbitrary")),
        name="peer_dense",
    )(xt, u, vt, nb, e1, rb, e2)


def _ple_body(h_ref, yt_ref, g_ref, wg_ref, p_ref, wp_ref, fg_ref, o_ref, *, final):
    h = h_ref[...] + yt_ref[...].T
    gate = jax.nn.sigmoid(_dot(_rms(h, g_ref[...]).astype(BF16), wg_ref[...]))
    out = h + gate * _dot(p_ref[...].astype(BF16), wp_ref[...])
    o_ref[...] = _rms(out, fg_ref[...]) if final else out


def _ple(h, yt, g, wg, p, wp, fg, final, tm=512):
    t, d = h.shape
    full = lambda a: pl.BlockSpec(a.shape, lambda i: (0, 0))
    return pl.pallas_call(
        functools.partial(_ple_body, final=final),
        grid=(t // tm,),
        in_specs=[pl.BlockSpec((tm, d), lambda i: (i, 0)), pl.BlockSpec((d, tm), lambda i: (0, i)), full(g), full(wg),
                  pl.BlockSpec((tm, p.shape[1]), lambda i: (i, 0)), full(wp), full(fg)],
        out_specs=pl.BlockSpec((tm, d), lambda i: (i, 0)),
        out_shape=jax.ShapeDtypeStruct((t, d), F32),
        compiler_params=_cparams(("parallel",)),
        name="ple",
    )(h, yt, g, wg, p, wp, fg)


def _cast_transpose_body(x_ref, o_ref):
    o_ref[...] = x_ref[...].T.astype(o_ref.dtype)


def _cast_transpose(x, layer, dtype, tile=1024):
    _, r, c = x.shape
    return pl.pallas_call(
        _cast_transpose_body,
        grid=(r // tile, c // tile),
        in_specs=[pl.BlockSpec((None, tile, tile), lambda i, j: (layer, i, j))],
        out_specs=pl.BlockSpec((tile, tile), lambda i, j: (j, i)),
        out_shape=jax.ShapeDtypeStruct((c, r), dtype),
        compiler_params=_cparams(("parallel", "parallel")),
        name="cast_transpose",
    )(x)


def _rot_half_cols(w):
    half = w.shape[-1] // 2
    return jnp.concatenate([-w[..., half:], w[..., :half]], axis=-1)


def _pad_cols(w, width):
    return jnp.pad(w, [(0, 0)] * (w.ndim - 1) + [(0, width - w.shape[-1])])


def _prep_w_in(w_in):
    offs = [0]
    for s in IN_SIZES:
        offs.append(offs[-1] + s)
    (d_q, d_k, d_v, g_q, g_f, g_v, g_g, m_cq, m_ckv, m_kr,
     l_q, l_k, l_v, l_i, l_f, l_o) = (w_in[..., offs[i]:offs[i + 1]] for i in range(len(IN_SIZES)))
    pieces = [d_q, d_k, d_v, g_q, g_f, g_v, g_g, l_v, l_o, m_ckv, l_q, l_k, m_cq,
              _pad_cols(m_kr, LANES), _pad_cols(_rot_half_cols(m_kr), LANES),
              _pad_cols(jnp.concatenate([l_i, l_f], axis=-1), LANES)]
    return jnp.concatenate(pieces, axis=-1)


def _prep_mla(w_uq, w_ukv):
    nl = w_uq.shape[0]
    wq = w_uq.reshape(nl, MLA_Q_RANK, HEADS, MLA_NOPE + MLA_ROPE)
    rope = wq[..., MLA_NOPE:]
    wq_all = jnp.concatenate([
        wq[..., :MLA_NOPE].reshape(nl, MLA_Q_RANK, MIX_W),
        _pad_cols(rope, LANES).reshape(nl, MLA_Q_RANK, MIX_W),
        _pad_cols(_rot_half_cols(rope), LANES).reshape(nl, MLA_Q_RANK, MIX_W)], axis=-1)
    wkv = w_ukv.reshape(nl, MLA_KV_RANK, HEADS, MLA_NOPE + HEAD_V)
    wkv_all = jnp.concatenate([wkv[..., :MLA_NOPE].reshape(nl, MLA_KV_RANK, MIX_W),
                               wkv[..., MLA_NOPE:].reshape(nl, MLA_KV_RANK, MIX_W)], axis=-1)
    return wq_all.astype(BF16), wkv_all.astype(BF16)


def kernel(x, p, positions, ln_mix, w_in, diff_lambda, diff_norm, hgrn_lb_logits, hgrn_norm, mla_q_norm,
           mla_kv_norm, mla_w_uq, mla_w_ukv, mlstm_conv_w, mlstm_conv_b, mlstm_gate_b, mlstm_norm, group_gain,
           w_out, ln_ffn, peer_w_q, peer_subkeys, peer_u, peer_v, ln_ple, ple_w_gate, ple_w_proj, ln_final):
    batch, seq, d = x.shape
    depth = w_in.shape[0]
    t = batch * seq

    wq_all, wkv_all = _prep_mla(mla_w_uq, mla_w_ukv)
    subkeys_b = peer_subkeys.reshape(depth, PEER_HEADS * 2, PEER_KEYS, PEER_KEY_DIM).astype(BF16)

    lv = diff_lambda.astype(F32)
    lam = jnp.exp(jnp.sum(lv[:, 0] * lv[:, 1], axis=-1)) - jnp.exp(jnp.sum(lv[:, 2] * lv[:, 3], axis=-1))
    lbs = jnp.cumsum(jax.nn.softmax(hgrn_lb_logits.astype(F32), axis=0), axis=0)
    lbs = lbs - lbs[:1]
    hgrn_par = jnp.stack([jnp.maximum(jnp.log(lbs), NEG), jnp.log1p(-lbs), 1.0 - lbs, hgrn_norm], axis=1)
    hgrn_par = jnp.pad(hgrn_par, ((0, 0), (0, 4), (0, 0)))
    gate_b = _pad_cols(mlstm_gate_b.reshape(depth, 1, 2 * HEADS), LANES)
    half = MLA_ROPE // 2
    inv = ROPE_THETA ** (-jnp.arange(half, dtype=F32) / half)
    inv = _pad_cols(jnp.concatenate([inv, inv])[None, :], LANES)
    pos = positions.astype(F32).reshape(t, 1)
    row = lambda a: a.reshape(1, -1)

    h = x.reshape(t, d)
    for i in range(depth):
        lam_init = 0.8 - 0.6 * math.exp(-0.3 * i)
        z = _norm_mm(h, row(ln_mix[i]), _prep_w_in(w_in[i].astype(BF16)), F32, tm=512, tn=2048)
        y_diff = _diff_attn(z, (lam[i] + lam_init).reshape(1, 1), row(diff_norm[i]), batch, seq, lam_init)
        y_hgrn = _hgrn(z, hgrn_par[i], batch, seq)
        mq, mk, mv = _mla_proj(z, pos, inv, row(mla_q_norm[i]), row(mla_kv_norm[i]), wq_all[i], wkv_all[i])
        y_mla = _mla_attn(mq, mk, mv, batch, seq)
        qk = _conv_silu(z, mlstm_conv_w[i], row(mlstm_conv_b[i]), batch, seq)
        y_mlstm = _mlstm(qk, z, gate_b[i], row(mlstm_norm[i]), batch, seq)
        h = _out_proj(h, (y_diff, y_hgrn, y_mla, y_mlstm), row(group_gain[i]), w_out[i].astype(BF16))
        pq, xn = _norm_mm(h, row(ln_ffn[i]), peer_w_q[i].astype(BF16), BF16, tm=512, tn=2048, emit_xn=True)
        yt = _peer_dense(xn.T, peer_u[i].astype(BF16), _cast_transpose(peer_v, i, BF16),
                         *_peer_router(pq, subkeys_b[i]))
        h = _ple(h, yt, row(ln_ple[i]), ple_w_gate[i].astype(BF16), p[i].reshape(t, PLE_DIM),
                 ple_w_proj[i].astype(BF16), row(ln_final), final=(i == depth - 1))
    return h.reshape(batch, seq, d)
```

```python
import functools
import math

import jax
import jax.numpy as jnp
from jax import lax
from jax.experimental import pallas as pl
from jax.experimental.pallas import tpu as pltpu

F32 = jnp.float32
BF16 = jnp.bfloat16

D_MODEL = 2048
PLE_DIM = 256
RMS_EPS = 1e-6
ROPE_THETA = 10000.0
HEADS = 4
HEAD_V = 128
MIX_W = HEADS * HEAD_V
DIFF_QK = 64
HGRN_DK = 128
MLA_NOPE = 128
MLA_ROPE = 64
MLA_Q_RANK = 384
MLA_KV_RANK = 256
MLA_QK_PAD = 256
MLSTM_DQK = 64
MLSTM_CONV = 4
CHUNK = 128
HGRN_CHUNK = 64
SUB = 16
PEER_HEADS = 8
PEER_KEYS = 128
PEER_KEY_DIM = 128
PEER_TOPK = 16
PEER_EXPERTS = PEER_KEYS * PEER_KEYS
NEG = -1e30

VMEM_LIMIT = 56 * 1024 * 1024
LANES = 128
SUBLANES = 8

IN_SIZES = (512, 512, 512, 512, 512, 512, 512, 384, 256, 64, 256, 256, 512, 4, 4, 512)
C_DQ, C_DK, C_DV, C_GQ, C_GF, C_GV, C_GG, C_LV, C_LO = (i * 512 for i in range(9))
C_CKV, C_LQ, C_LK = 4608, 4864, 5120
C_CQ = 5376
C_KR, C_KRR, C_GATE = 5760, 5888, 6016
N_IN = 6144


def _cparams(sem):
    return pltpu.CompilerParams(dimension_semantics=sem, vmem_limit_bytes=VMEM_LIMIT)


def _rms(x, g):
    return x * lax.rsqrt(jnp.mean(x * x, axis=-1, keepdims=True) + RMS_EPS) * g


def _dot(a, b):
    return jnp.dot(a, b, preferred_element_type=F32)


def _dot_nt(a, b):
    return lax.dot_general(a, b, (((1,), (1,)), ((), ())), preferred_element_type=F32)


def _dot_tn(a, b):
    return lax.dot_general(a, b, (((0,), (0,)), ((), ())), preferred_element_type=F32)


def _log_sigmoid(x):
    return jnp.minimum(x, 0.0) - jnp.log1p(jnp.exp(-jnp.abs(x)))


def _cumsum_rows(x):
    row = lax.broadcasted_iota(jnp.int32, x.shape, 0)
    k = 1
    while k < x.shape[0]:
        x = x + jnp.where(row >= k, pltpu.roll(x, k, 0), 0.0)
        k *= 2
    return x


def _norm_mm_body(x_ref, g_ref, w_ref, o_ref, *rest, emit_xn):
    xn_ref = rest[-1]

    @pl.when(pl.program_id(1) == 0)
    def _():
        y = _rms(x_ref[...], g_ref[...]).astype(BF16)
        xn_ref[...] = y
        if emit_xn:
            rest[0][...] = y

    o_ref[...] = _dot(xn_ref[...], w_ref[...]).astype(o_ref.dtype)


def _norm_mm(x, g, w, out_dtype, tm, tn, emit_xn=False):
    t, d = x.shape
    n = w.shape[1]
    out_shape = [jax.ShapeDtypeStruct((t, n), out_dtype)]
    out_specs = [pl.BlockSpec((tm, tn), lambda i, j: (i, j))]
    if emit_xn:
        out_shape.append(jax.ShapeDtypeStruct((t, d), BF16))
        out_specs.append(pl.BlockSpec((tm, d), lambda i, j: (i, 0)))
    res = pl.pallas_call(
        functools.partial(_norm_mm_body, emit_xn=emit_xn),
        grid=(t // tm, n // tn),
        in_specs=[pl.BlockSpec((tm, d), lambda i, j: (i, 0)),
                  pl.BlockSpec((1, d), lambda i, j: (0, 0)),
                  pl.BlockSpec((d, tn), lambda i, j: (0, j))],
        out_specs=out_specs,
        out_shape=out_shape,
        scratch_shapes=[pltpu.VMEM((tm, d), BF16)],
        compiler_params=_cparams(("parallel", "arbitrary")),
        name="norm_mm",
    )(x, g, w)
    return res if emit_xn else res[0]


def _causal_mask(i, tq):
    row = i * tq + lax.broadcasted_iota(jnp.int32, (tq, (i + 1) * tq), 0)
    col = lax.broadcasted_iota(jnp.int32, (tq, (i + 1) * tq), 1)
    return col <= row


def _softmax_terms(s, mask):
    s = jnp.where(mask, s, NEG)
    e = jnp.exp(s - jnp.max(s, axis=-1, keepdims=True))
    return e, 1.0 / jnp.sum(e, axis=-1, keepdims=True)


def _per_query_block(nq, fn):
    for i in range(nq):
        pl.when(pl.program_id(2) == i)(functools.partial(fn, i))


def _diff_attn_body(lam_ref, q_ref, k_ref, v_ref, g_ref, o_ref, *, tq, nq, lam_init):
    def block(i):
        kv = (i + 1) * tq
        mask = _causal_mask(i, tq)
        q = q_ref[...] * (DIFF_QK ** -0.5)
        lane = lax.broadcasted_iota(jnp.int32, q.shape, 1)
        k = k_ref[0:kv, :].astype(BF16)
        e1, r1 = _softmax_terms(_dot_nt(jnp.where(lane < DIFF_QK, q, 0.0).astype(BF16), k), mask)
        e2, r2 = _softmax_terms(_dot_nt(jnp.where(lane >= DIFF_QK, q, 0.0).astype(BF16), k), mask)
        a = e1 * r1 - e2 * (lam_ref[0, 0] * r2)
        o = _dot(a.astype(BF16), v_ref[0:kv, :].astype(BF16))
        o_ref[...] = _rms(o, g_ref[...]) * (1.0 - lam_init)

    _per_query_block(nq, block)


def _diff_attn(z, lam, g, batch, seq, lam_init, tq=256):
    nq = seq // tq
    return pl.pallas_call(
        functools.partial(_diff_attn_body, tq=tq, nq=nq, lam_init=lam_init),
        grid=(batch, HEADS, nq),
        in_specs=[pl.BlockSpec(memory_space=pltpu.SMEM),
                  pl.BlockSpec((tq, LANES), lambda b, h, i: (b * nq + i, C_DQ // LANES + h)),
                  pl.BlockSpec((seq, LANES), lambda b, h, i: (b, C_DK // LANES + h)),
                  pl.BlockSpec((seq, LANES), lambda b, h, i: (b, C_DV // LANES + h)),
                  pl.BlockSpec((1, LANES), lambda b, h, i: (0, h))],
        out_specs=pl.BlockSpec((tq, LANES), lambda b, h, i: (b * nq + i, h)),
        out_shape=jax.ShapeDtypeStruct((batch * seq, MIX_W), F32),
        compiler_params=_cparams(("parallel", "parallel", "arbitrary")),
        name="diff_attn",
    )(lam, z, z, z, g)


def _mla_proj_body(cq_ref, ckv_ref, kr_ref, krr_ref, pos_ref, inv_ref, qn_ref, kvn_ref,
                   wq_ref, wkv_ref, q_ref, k_ref, v_ref):
    qa = _dot(_rms(cq_ref[...], qn_ref[...]).astype(BF16), wq_ref[...])
    kva = _dot(_rms(ckv_ref[...], kvn_ref[...]).astype(BF16), wkv_ref[...])
    ang = pos_ref[...] * inv_ref[...]
    cos, sin = jnp.cos(ang), jnp.sin(ang)
    k_rope = (kr_ref[...] * cos + krr_ref[...] * sin).astype(BF16)
    scale = (MLA_NOPE + MLA_ROPE) ** -0.5
    for h in range(HEADS):
        lo, hi = h * LANES, (h + 1) * LANES
        q_rope = qa[:, MIX_W + lo:MIX_W + hi] * cos + qa[:, 2 * MIX_W + lo:2 * MIX_W + hi] * sin
        q_ref[:, 2 * lo:2 * lo + LANES] = (qa[:, lo:hi] * scale).astype(BF16)
        q_ref[:, 2 * lo + LANES:2 * hi] = (q_rope * scale).astype(BF16)
        k_ref[:, 2 * lo:2 * lo + LANES] = kva[:, lo:hi].astype(BF16)
        k_ref[:, 2 * lo + LANES:2 * hi] = k_rope
    v_ref[...] = kva[:, MIX_W:].astype(BF16)


def _mla_proj(z, pos, inv, qn, kvn, wq, wkv, tm=512):
    t = z.shape[0]
    row = lambda c, w: pl.BlockSpec((tm, w), lambda i: (i, c // w))
    full = lambda a: pl.BlockSpec(a.shape, lambda i: (0, 0))
    return pl.pallas_call(
        _mla_proj_body,
        grid=(t // tm,),
        in_specs=[row(C_CQ, MLA_Q_RANK), row(C_CKV, MLA_KV_RANK), row(C_KR, LANES), row(C_KRR, LANES),
                  pl.BlockSpec((tm, 1), lambda i: (i, 0)), full(inv), full(qn), full(kvn), full(wq), full(wkv)],
        out_specs=[pl.BlockSpec((tm, HEADS * MLA_QK_PAD), lambda i: (i, 0)),
                   pl.BlockSpec((tm, HEADS * MLA_QK_PAD), lambda i: (i, 0)),
                   pl.BlockSpec((tm, MIX_W), lambda i: (i, 0))],
        out_shape=[jax.ShapeDtypeStruct((t, HEADS * MLA_QK_PAD), BF16),
                   jax.ShapeDtypeStruct((t, HEADS * MLA_QK_PAD), BF16),
                   jax.ShapeDtypeStruct((t, MIX_W), BF16)],
        compiler_params=_cparams(("parallel",)),
        name="mla_proj",
    )(z, z, z, z, pos, inv, qn, kvn, wq, wkv)


def _mla_attn_body(q_ref, k_ref, v_ref, o_ref, *, tq, nq):
    def block(i):
        kv = (i + 1) * tq
        e, r = _softmax_terms(_dot_nt(q_ref[...], k_ref[0:kv, :]), _causal_mask(i, tq))
        o_ref[...] = _dot((e * r).astype(BF16), v_ref[0:kv, :])

    _per_query_block(nq, block)


def _mla_attn(q, k, v, batch, seq, tq=256):
    nq = seq // tq
    return pl.pallas_call(
        functools.partial(_mla_attn_body, tq=tq, nq=nq),
        grid=(batch, HEADS, nq),
        in_specs=[pl.BlockSpec((tq, MLA_QK_PAD), lambda b, h, i: (b * nq + i, h)),
                  pl.BlockSpec((seq, MLA_QK_PAD), lambda b, h, i: (b, h)),
                  pl.BlockSpec((seq, LANES), lambda b, h, i: (b, h))],
        out_specs=pl.BlockSpec((tq, LANES), lambda b, h, i: (b * nq + i, h)),
        out_shape=jax.ShapeDtypeStruct((batch * seq, MIX_W), F32),
        compiler_params=_cparams(("parallel", "parallel", "arbitrary")),
        name="mla_attn",
    )(q, k, v)


def _hgrn_body(q_ref, f_ref, v_ref, g_ref, par_ref, o_ref, st_ref):
    @pl.when(pl.program_id(1) == 0)
    def _():
        st_ref[...] = jnp.zeros_like(st_ref)

    sub_row = lax.broadcasted_iota(jnp.int32, (SUB, SUB), 0)
    sub_col = lax.broadcasted_iota(jnp.int32, (SUB, SUB), 1)
    for h in range(HEADS):
        sl = slice(h * LANES, (h + 1) * LANES)
        log_lb, log_1m_lb, one_m_lb, gain = (par_ref[r:r + 1, sl] for r in range(4))
        zf = f_ref[:, sl]
        q = q_ref[:, sl] * (HGRN_DK ** -0.5)
        v = v_ref[:, sl].astype(BF16)
        k = one_m_lb * jax.nn.sigmoid(-zf)
        lf_a, lf_b = log_lb, log_1m_lb + _log_sigmoid(zf)
        log_f = jnp.maximum(lf_a, lf_b) + jnp.log1p(jnp.exp(-jnp.abs(lf_a - lf_b)))
        b = _cumsum_rows(log_f)
        st = st_ref[h]
        o = _dot_nt((q * jnp.exp(b)).astype(BF16), st.astype(BF16))
        parts = []
        for i in range(HGRN_CHUNK // SUB):
            lo = i * SUB
            qi, bi, ki, vi = q[lo:lo + SUB], b[lo:lo + SUB], k[lo:lo + SUB], v[lo:lo + SUB]
            sc = jnp.zeros((SUB, SUB), F32)
            for s in range(SUB):
                dec = jnp.exp(jnp.minimum(bi - bi[s:s + 1], 0.0))
                col = jnp.sum(qi * ki[s:s + 1] * dec, axis=-1, keepdims=True)
                sc = sc + jnp.where(sub_col == s, col, 0.0)
            sc = jnp.where(sub_col <= sub_row, sc, 0.0)
            part = _dot(sc.astype(BF16), vi)
            if i > 0:
                b0 = b[lo - 1:lo]
                qe = (qi * jnp.exp(bi - b0)).astype(BF16)
                ke = (k[:lo] * jnp.exp(b0 - b[:lo])).astype(BF16)
                part = part + _dot(_dot_nt(qe, ke).astype(BF16), v[:lo])
            parts.append(part)
        o = o + jnp.concatenate(parts, axis=0)
        b_last = b[HGRN_CHUNK - 1:HGRN_CHUNK]
        st_ref[h] = st * jnp.exp(b_last) + _dot_tn(v, (k * jnp.exp(b_last - b)).astype(BF16))
        o_ref[:, sl] = _rms(o, gain) * jax.nn.silu(g_ref[:, sl])


def _hgrn(z, par, batch, seq):
    nc = seq // HGRN_CHUNK
    blk = lambda c: pl.BlockSpec((HGRN_CHUNK, MIX_W), lambda b, i: (b * nc + i, c // MIX_W))
    return pl.pallas_call(
        _hgrn_body,
        grid=(batch, nc),
        in_specs=[blk(C_GQ), blk(C_GF), blk(C_GV), blk(C_GG), pl.BlockSpec((8, MIX_W), lambda b, i: (0, 0))],
        out_specs=pl.BlockSpec((HGRN_CHUNK, MIX_W), lambda b, i: (b * nc + i, 0)),
        out_shape=jax.ShapeDtypeStruct((batch * seq, MIX_W), F32),
        scratch_shapes=[pltpu.VMEM((HEADS, HEAD_V, HGRN_DK), F32)],
        compiler_params=_cparams(("parallel", "arbitrary")),
        name="hgrn2",
    )(z, z, z, z, par)


def _conv_silu_body(x_ref, w_ref, b_ref, o_ref):
    x = x_ref[...]
    row = lax.broadcasted_iota(jnp.int32, x.shape, 0)
    y = x * w_ref[MLSTM_CONV - 1:MLSTM_CONV] + b_ref[...]
    for k in range(1, MLSTM_CONV):
        y = y + jnp.where(row >= k, pltpu.roll(x, k, 0), 0.0) * w_ref[MLSTM_CONV - 1 - k:MLSTM_CONV - k]
    o_ref[...] = jax.nn.silu(y)


def _conv_silu(z, w, b, batch, seq):
    wd = HEADS * MLSTM_DQK
    return pl.pallas_call(
        _conv_silu_body,
        grid=(batch, 2),
        in_specs=[pl.BlockSpec((seq, wd), lambda bi, j: (bi, C_LQ // wd + j)),
                  pl.BlockSpec((MLSTM_CONV, wd), lambda bi, j: (0, j)),
                  pl.BlockSpec((1, wd), lambda bi, j: (0, j))],
        out_specs=pl.BlockSpec((seq, wd), lambda bi, j: (bi, j)),
        out_shape=jax.ShapeDtypeStruct((batch * seq, 2 * wd), F32),
        compiler_params=_cparams(("parallel", "parallel")),
        name="conv_silu",
    )(z, w, b)


def _mlstm_body(qk_ref, v_ref, gate_ref, og_ref, gb_ref, norm_ref, o_ref, c_ref, m_ref):
    @pl.when(pl.program_id(1) == 0)
    def _():
        c_ref[...] = jnp.zeros_like(c_ref)
        m_ref[...] = jnp.zeros_like(m_ref)

    g = gate_ref[...] + gb_ref[...]
    lane = lax.broadcasted_iota(jnp.int32, g.shape, 1)
    gt = jnp.where(lane < HEADS, g, _cumsum_rows(_log_sigmoid(g)))
    gt_t = gt.T
    row = lax.broadcasted_iota(jnp.int32, (CHUNK, CHUNK), 0)
    col = lax.broadcasted_iota(jnp.int32, (CHUNK, CHUNK), 1)
    ones_col = jnp.where(lane == 0, 1.0, 0.0).astype(BF16)
    half_row = lax.broadcasted_iota(jnp.int32, (LANES, 1), 0) < MLSTM_DQK
    for p in range(HEADS // 2):
        q_t = qk_ref[:, p * LANES:(p + 1) * LANES] * (MLSTM_DQK ** -0.5)
        k_t = qk_ref[:, HEADS * MLSTM_DQK + p * LANES:HEADS * MLSTM_DQK + (p + 1) * LANES]
        c_old = c_ref[p]
        c_bf = c_old.astype(BF16)
        upd = jnp.zeros_like(c_old)
        carries = []
        for u in range(2):
            h = 2 * p + u
            mine = (lane >= u * MLSTM_DQK) & (lane < (u + 1) * MLSTM_DQK)
            li_col, b_col = gt[:, h:h + 1], gt[:, HEADS + h:HEADS + h + 1]
            li_row, b_row = gt_t[h:h + 1, :], gt_t[HEADS + h:HEADS + h + 1, :]
            m_st = m_ref[h:h + 1, 0:1]
            dmat = jnp.where(col <= row, b_col - b_row + li_row, NEG)
            inter_log = b_col + m_st
            m_t = jnp.maximum(inter_log, jnp.max(dmat, axis=-1, keepdims=True))
            qm = jnp.where(mine, q_t, 0.0).astype(BF16)
            s = _dot_nt(qm, k_t.astype(BF16)) * jnp.exp(dmat - m_t)
            v_aug = jnp.concatenate([v_ref[:, h * LANES:(h + 1) * LANES].astype(BF16), ones_col], axis=1)
            out = jnp.exp(inter_log - m_t) * _dot(qm, c_bf) + _dot(s.astype(BF16), v_aug)
            hid = out[:, :HEAD_V] / jnp.maximum(jnp.abs(out[:, HEAD_V:HEAD_V + 1]), jnp.exp(-m_t))
            sl = slice(h * LANES, (h + 1) * LANES)
            o_ref[:, sl] = _rms(hid, norm_ref[:, sl]) * jax.nn.sigmoid(og_ref[:, sl])
            b_last = b_col[CHUNK - 1:CHUNK]
            log_s = b_last - b_col + li_col
            m_new = jnp.maximum(b_last + m_st, jnp.max(log_s, axis=0, keepdims=True))
            carries.append(jnp.exp(b_last + m_st - m_new))
            kw = (jnp.where(mine, k_t, 0.0) * jnp.exp(log_s - m_new)).astype(BF16)
            upd = upd + _dot_tn(kw, v_aug)
            m_ref[h:h + 1, :] = jnp.broadcast_to(m_new, (1, LANES))
        c_ref[p] = jnp.where(half_row, carries[0], carries[1]) * c_old + upd


def _mlstm(qk, z, gb, norm, batch, seq):
    nc = seq // CHUNK
    return pl.pallas_call(
        _mlstm_body,
        grid=(batch, nc),
        in_specs=[pl.BlockSpec((CHUNK, 2 * HEADS * MLSTM_DQK), lambda b, i: (b * nc + i, 0)),
                  pl.BlockSpec((CHUNK, MIX_W), lambda b, i: (b * nc + i, C_LV // MIX_W)),
                  pl.BlockSpec((CHUNK, LANES), lambda b, i: (b * nc + i, C_GATE // LANES)),
                  pl.BlockSpec((CHUNK, MIX_W), lambda b, i: (b * nc + i, C_LO // MIX_W)),
                  pl.BlockSpec((1, LANES), lambda b, i: (0, 0)),
                  pl.BlockSpec((1, MIX_W), lambda b, i: (0, 0))],
        out_specs=pl.BlockSpec((CHUNK, MIX_W), lambda b, i: (b * nc + i, 0)),
        out_shape=jax.ShapeDtypeStruct((batch * seq, MIX_W), F32),
        scratch_shapes=[pltpu.VMEM((HEADS // 2, LANES, 2 * HEAD_V), F32), pltpu.VMEM((8, LANES), F32)],
        compiler_params=_cparams(("parallel", "arbitrary")),
        name="mlstm",
    )(qk, z, z, z, gb, norm)


def _out_proj_body(h_ref, y0_ref, y1_ref, y2_ref, y3_ref, g_ref, w_ref, o_ref):
    acc = h_ref[...]
    for i, y_ref in enumerate((y0_ref, y1_ref, y2_ref, y3_ref)):
        sl = slice(i * MIX_W, (i + 1) * MIX_W)
        acc = acc + _dot((y_ref[...] * g_ref[:, sl]).astype(BF16), w_ref[sl, :])
    o_ref[...] = acc


def _out_proj(h, ys, g, w, tm=512):
    t, d = h.shape
    yspec = pl.BlockSpec((tm, MIX_W), lambda i: (i, 0))
    return pl.pallas_call(
        _out_proj_body,
        grid=(t // tm,),
        in_specs=[pl.BlockSpec((tm, d), lambda i: (i, 0)), yspec, yspec, yspec, yspec,
                  pl.BlockSpec((1, HEADS * MIX_W), lambda i: (0, 0)),
                  pl.BlockSpec((HEADS * MIX_W, d), lambda i: (0, 0))],
        out_specs=pl.BlockSpec((tm, d), lambda i: (i, 0)),
        out_shape=jax.ShapeDtypeStruct((t, d), F32),
        compiler_params=_cparams(("parallel",)),
        name="out_proj",
    )(h, *ys, g, w)


def _ranked_top(x, n, dst_ref):
    rank = jnp.full(x.shape, float(n), F32)
    for r in range(n):
        mx = jnp.max(x, axis=0, keepdims=True)
        dst_ref[r:r + 1, :] = mx
        eq = x == mx
        rank = jnp.where(eq, float(r), rank)
        x = jnp.where(eq, NEG, x)
    return rank


def _kth_and_next(x, k):
    cum = jnp.zeros((1, x.shape[1]), F32)
    kth = jnp.full((1, x.shape[1]), NEG, F32)
    nxt = kth
    for _ in range(k + 1):
        mx = jnp.max(x, axis=0, keepdims=True)
        eq = x == mx
        new = cum + jnp.sum(eq.astype(F32), axis=0, keepdims=True)
        kth = jnp.where((cum < k) & (new >= k), mx, kth)
        nxt = jnp.where((cum < k + 1) & (new >= k + 1), mx, nxt)
        cum = new
        x = jnp.where(eq, NEG, x)
    return kth, nxt


PAIR_COLS = tuple((PEER_TOPK + 1) // (i + 1) for i in range(PEER_TOPK + 1))
PAIR_OFFS = tuple(sum(PAIR_COLS[:i]) for i in range(PEER_TOPK + 2))
PAIR_ROWS = -(-PAIR_OFFS[-1] // SUBLANES) * SUBLANES


def _pack_halves(x):
    bits = pltpu.bitcast(x.astype(BF16).astype(F32), jnp.uint32)
    half = x.shape[0] // 2
    return (bits[half:] & jnp.uint32(0xFFFF0000)) | (bits[:half] >> 16)


def _peer_router_body(q_ref, sk_ref, nb_ref, e1_ref, rb_ref, e2_ref, a_ref, b_ref, g_ref):
    tt = q_ref.shape[0]
    for h in range(PEER_HEADS):
        s1 = _dot_nt(sk_ref[2 * h], q_ref[:, 2 * h * LANES:(2 * h + 1) * LANES])
        s2 = _dot_nt(sk_ref[2 * h + 1], q_ref[:, (2 * h + 1) * LANES:(2 * h + 2) * LANES])
        rank_a = _ranked_top(s1, PEER_TOPK + 1, a_ref)
        rank_b = _ranked_top(s2, PEER_TOPK + 1, b_ref)
        g_ref[PAIR_ROWS - SUBLANES:PAIR_ROWS, :] = jnp.full((SUBLANES, tt), NEG, F32)
        for i, n in enumerate(PAIR_COLS):
            g_ref[PAIR_OFFS[i]:PAIR_OFFS[i] + n, :] = a_ref[i:i + 1, :] + b_ref[0:n, :]
        pair_sums = g_ref[...]
        kth, nxt = _kth_and_next(pair_sums, PEER_TOPK)
        thr = 0.5 * (kth + nxt)
        a_top, b_top = a_ref[0:1, :], b_ref[0:1, :]
        chosen = pair_sums >= thr
        zsum = jnp.sum(jnp.where(chosen, jnp.exp(pair_sums - (a_top + b_top)), 0.0), axis=0, keepdims=True)
        ones = jnp.where(chosen, 1.0, 0.0)
        count_a = jnp.zeros_like(s1)
        for i in range(PEER_TOPK):
            n_i = jnp.sum(ones[PAIR_OFFS[i]:PAIR_OFFS[i + 1]], axis=0, keepdims=True)
            count_a = jnp.where(rank_a == float(i), n_i, count_a)
        e1 = jnp.exp(s1 - a_top).astype(BF16).astype(F32)
        for q in range(tt // LANES):
            nb_ref[q, h] = count_a[:, q * LANES:(q + 1) * LANES]
            e1_ref[q, h] = e1[:, q * LANES:(q + 1) * LANES]
        rb_ref[h] = _pack_halves(rank_b)
        e2_ref[h] = _pack_halves(jnp.exp(s2 - b_top) * (0.5 / zsum))


def _peer_router(q, sk, tt=256):
    t = q.shape[0]
    row_spec = pl.BlockSpec((tt // LANES, PEER_HEADS, PEER_KEYS, LANES), lambda i: (i, 0, 0, 0))
    row_shape = jax.ShapeDtypeStruct((t // LANES, PEER_HEADS, PEER_KEYS, LANES), F32)
    key_spec = pl.BlockSpec((PEER_HEADS, PEER_KEYS // 2, tt), lambda i: (0, 0, i))
    key_shape = jax.ShapeDtypeStruct((PEER_HEADS, PEER_KEYS // 2, t), jnp.uint32)
    return pl.pallas_call(
        _peer_router_body,
        grid=(t // tt,),
        in_specs=[pl.BlockSpec((tt, q.shape[1]), lambda i: (i, 0)),
                  pl.BlockSpec(sk.shape, lambda i: (0, 0, 0))],
        out_specs=[row_spec, row_spec, key_spec, key_spec],
        out_shape=[row_shape, row_shape, key_shape, key_shape],
        scratch_shapes=[pltpu.VMEM((24, tt), F32), pltpu.VMEM((24, tt), F32),
                        pltpu.VMEM((PAIR_ROWS, tt), F32)],
        compiler_params=_cparams(("parallel",)),
        name="peer_router",
    )(q, sk)


PEER_MROWS = 256


def _peer_dense_body(xt_ref, u_ref, vt_ref, nb_ref, e1_ref, rb_ref, e2_ref, o_ref, act_ref, wa_ref, *, rows):
    @pl.when(pl.program_id(1) == 0)
    def _():
        o_ref[...] = jnp.zeros_like(o_ref)

    et, tt = act_ref.shape
    half = PEER_KEYS // 2
    for m in range(et // PEER_MROWS):
        ms = slice(m * PEER_MROWS, (m + 1) * PEER_MROWS)
        act_ref[ms, :] = _dot(u_ref[ms, :], xt_ref[...])

    def gelu2(z):
        return z + z * lax.erf(z * (2.0 ** -0.5))

    for a in range(rows):
        for q in range(tt // LANES):
            tok = slice(q * LANES, (q + 1) * LANES)

            def row(ref, h):
                bits = pltpu.bitcast(jnp.broadcast_to(ref[q, h, a:a + 1, :], (SUBLANES, LANES)), jnp.uint32)
                return pltpu.bitcast((bits & jnp.uint32(0xFFFF0000)) | (bits >> 16), BF16)

            nbs = [row(nb_ref, h) for h in range(PEER_HEADS)]
            e1s = [row(e1_ref, h) for h in range(PEER_HEADS)]
            for s in range(half // SUBLANES):
                ks = slice(s * SUBLANES, (s + 1) * SUBLANES)
                w = None
                for h in range(PEER_HEADS):
                    rb = pltpu.bitcast(rb_ref[h, ks, tok], BF16)
                    e2 = pltpu.bitcast(e2_ref[h, ks, tok], BF16)
                    t = jnp.where(rb < nbs[h], e2 * e1s[h], 0.0)
                    w = t if w is None else w + t
                wbits = pltpu.bitcast(w, jnp.uint32)
                lo = slice(a * PEER_KEYS + s * SUBLANES, a * PEER_KEYS + (s + 1) * SUBLANES)
                hi = slice(a * PEER_KEYS + half + s * SUBLANES, a * PEER_KEYS + half + (s + 1) * SUBLANES)
                wa_ref[lo, tok] = (pltpu.bitcast(wbits << 16, F32) * gelu2(act_ref[lo, tok])).astype(BF16)
                wa_ref[hi, tok] = (pltpu.bitcast(wbits & jnp.uint32(0xFFFF0000), F32)
                                   * gelu2(act_ref[hi, tok])).astype(BF16)
    for m in range(o_ref.shape[0] // PEER_MROWS):
        ms = slice(m * PEER_MROWS, (m + 1) * PEER_MROWS)
        o_ref[ms, :] += _dot(vt_ref[ms, :], wa_ref[...])


def _peer_dense(xt, u, vt, nb, e1, rb, e2, tt=512, rows=8):
    d, t = xt.shape
    et = rows * PEER_KEYS
    tok_blk = pl.BlockSpec((d, tt), lambda i, j: (0, i))
    row_blk = pl.BlockSpec((tt // LANES, PEER_HEADS, rows, LANES), lambda i, j: (i, 0, j, 0))
    key_blk = pl.BlockSpec((PEER_HEADS, PEER_KEYS // 2, tt), lambda i, j: (0, 0, i))
    return pl.pallas_call(
        functools.partial(_peer_dense_body, rows=rows),
        grid=(t // tt, PEER_EXPERTS // et),
        in_specs=[tok_blk, pl.BlockSpec((et, d), lambda i, j: (j, 0)), pl.BlockSpec((d, et), lambda i, j: (0, j)),
                  row_blk, row_blk, key_blk, key_blk],
        out_specs=tok_blk,
        out_shape=jax.ShapeDtypeStruct((d, t), F32),
        scratch_shapes=[pltpu.VMEM((et, tt), F32), pltpu.VMEM((et, tt), BF16)],
        compiler_params=_cparams(("parallel", "arbitrary")),
        name="peer_dense",
    )(xt, u, vt, nb, e1, rb, e2)


def _ple_body(h_ref, yt_ref, g_ref, wg_ref, p_ref, wp_ref, fg_ref, o_ref, *, final):
    h = h_ref[...] + yt_ref[...].T
    gate = jax.nn.sigmoid(_dot(_rms(h, g_ref[...]).astype(BF16), wg_ref[...]))
    out = h + gate * _dot(p_ref[...].astype(BF16), wp_ref[...])
    o_ref[...] = _rms(out, fg_ref[...]) if final else out


def _ple(h, yt, g, wg, p, wp, fg, final, tm=512):
    t, d = h.shape
    full = lambda a: pl.BlockSpec(a.shape, lambda i: (0, 0))
    return pl.pallas_call(
        functools.partial(_ple_body, final=final),
        grid=(t // tm,),
        in_specs=[pl.BlockSpec((tm, d), lambda i: (i, 0)), pl.BlockSpec((d, tm), lambda i: (0, i)), full(g), full(wg),
                  pl.BlockSpec((tm, p.shape[1]), lambda i: (i, 0)), full(wp), full(fg)],
        out_specs=pl.BlockSpec((tm, d), lambda i: (i, 0)),
        out_shape=jax.ShapeDtypeStruct((t, d), F32),
        compiler_params=_cparams(("parallel",)),
        name="ple",
    )(h, yt, g, wg, p, wp, fg)


def _cast_body(x_ref, o_ref):
    o_ref[...] = x_ref[...].astype(o_ref.dtype)


def _cast_layer(x, layer, dtype, rows):
    _, r, c = x.shape
    return pl.pallas_call(
        _cast_body,
        grid=(r // rows,),
        in_specs=[pl.BlockSpec((None, rows, c), lambda i: (layer, i, 0))],
        out_specs=pl.BlockSpec((rows, c), lambda i: (i, 0)),
        out_shape=jax.ShapeDtypeStruct((r, c), dtype),
        compiler_params=_cparams(("parallel",)),
        name="cast_layer",
    )(x)


def _cast_transpose_body(x_ref, o_ref):
    o_ref[...] = x_ref[...].T.astype(o_ref.dtype)


def _cast_transpose(x, layer, dtype, tile=1024):
    _, r, c = x.shape
    return pl.pallas_call(
        _cast_transpose_body,
        grid=(r // tile, c // tile),
        in_specs=[pl.BlockSpec((None, tile, tile), lambda i, j: (layer, i, j))],
        out_specs=pl.BlockSpec((tile, tile), lambda i, j: (j, i)),
        out_shape=jax.ShapeDtypeStruct((c, r), dtype),
        compiler_params=_cparams(("parallel", "parallel")),
        name="cast_transpose",
    )(x)


def _rot_half_cols(w):
    half = w.shape[-1] // 2
    return jnp.concatenate([-w[..., half:], w[..., :half]], axis=-1)


def _pad_cols(w, width):
    return jnp.pad(w, [(0, 0)] * (w.ndim - 1) + [(0, width - w.shape[-1])])


def _prep_w_in(w_in):
    offs = [0]
    for s in IN_SIZES:
        offs.append(offs[-1] + s)
    (d_q, d_k, d_v, g_q, g_f, g_v, g_g, m_cq, m_ckv, m_kr,
     l_q, l_k, l_v, l_i, l_f, l_o) = (w_in[..., offs[i]:offs[i + 1]] for i in range(len(IN_SIZES)))
    pieces = [d_q, d_k, d_v, g_q, g_f, g_v, g_g, l_v, l_o, m_ckv, l_q, l_k, m_cq,
              _pad_cols(m_kr, LANES), _pad_cols(_rot_half_cols(m_kr), LANES),
              _pad_cols(jnp.concatenate([l_i, l_f], axis=-1), LANES)]
    return jnp.concatenate(pieces, axis=-1)


def _prep_mla(w_uq, w_ukv):
    nl = w_uq.shape[0]
    wq = w_uq.reshape(nl, MLA_Q_RANK, HEADS, MLA_NOPE + MLA_ROPE)
    rope = wq[..., MLA_NOPE:]
    wq_all = jnp.concatenate([
        wq[..., :MLA_NOPE].reshape(nl, MLA_Q_RANK, MIX_W),
        _pad_cols(rope, LANES).reshape(nl, MLA_Q_RANK, MIX_W),
        _pad_cols(_rot_half_cols(rope), LANES).reshape(nl, MLA_Q_RANK, MIX_W)], axis=-1)
    wkv = w_ukv.reshape(nl, MLA_KV_RANK, HEADS, MLA_NOPE + HEAD_V)
    wkv_all = jnp.concatenate([wkv[..., :MLA_NOPE].reshape(nl, MLA_KV_RANK, MIX_W),
                               wkv[..., MLA_NOPE:].reshape(nl, MLA_KV_RANK, MIX_W)], axis=-1)
    return wq_all.astype(BF16), wkv_all.astype(BF16)


def kernel(x, p, positions, ln_mix, w_in, diff_lambda, diff_norm, hgrn_lb_logits, hgrn_norm, mla_q_norm,
           mla_kv_norm, mla_w_uq, mla_w_ukv, mlstm_conv_w, mlstm_conv_b, mlstm_gate_b, mlstm_norm, group_gain,
           w_out, ln_ffn, peer_w_q, peer_subkeys, peer_u, peer_v, ln_ple, ple_w_gate, ple_w_proj, ln_final):
    batch, seq, d = x.shape
    depth = w_in.shape[0]
    t = batch * seq

    wq_all, wkv_all = _prep_mla(mla_w_uq, mla_w_ukv)
    subkeys_b = peer_subkeys.reshape(depth, PEER_HEADS * 2, PEER_KEYS, PEER_KEY_DIM).astype(BF16)

    lv = diff_lambda.astype(F32)
    lam = jnp.exp(jnp.sum(lv[:, 0] * lv[:, 1], axis=-1)) - jnp.exp(jnp.sum(lv[:, 2] * lv[:, 3], axis=-1))
    lbs = jnp.cumsum(jax.nn.softmax(hgrn_lb_logits.astype(F32), axis=0), axis=0)
    lbs = lbs - lbs[:1]
    hgrn_par = jnp.stack([jnp.maximum(jnp.log(lbs), NEG), jnp.log1p(-lbs), 1.0 - lbs, hgrn_norm], axis=1)
    hgrn_par = jnp.pad(hgrn_par, ((0, 0), (0, 4), (0, 0)))
    gate_b = _pad_cols(mlstm_gate_b.reshape(depth, 1, 2 * HEADS), LANES)
    half = MLA_ROPE // 2
    inv = ROPE_THETA ** (-jnp.arange(half, dtype=F32) / half)
    inv = _pad_cols(jnp.concatenate([inv, inv])[None, :], LANES)
    pos = positions.astype(F32).reshape(t, 1)
    row = lambda a: a.reshape(1, -1)

    h = x.reshape(t, d)
    for i in range(depth):
        lam_init = 0.8 - 0.6 * math.exp(-0.3 * i)
        cast = lambda a, rows: _cast_layer(a, i, BF16, rows)
        z = _norm_mm(h, row(ln_mix[i]), _prep_w_in(cast(w_in, 256)), F32, tm=512, tn=2048)
        y_diff = _diff_attn(z, (lam[i] + lam_init).reshape(1, 1), row(diff_norm[i]), batch, seq, lam_init)
        y_hgrn = _hgrn(z, hgrn_par[i], batch, seq)
        mq, mk, mv = _mla_proj(z, pos, inv, row(mla_q_norm[i]), row(mla_kv_norm[i]), wq_all[i], wkv_all[i])
        y_mla = _mla_attn(mq, mk, mv, batch, seq)
        qk = _conv_silu(z, mlstm_conv_w[i], row(mlstm_conv_b[i]), batch, seq)
        y_mlstm = _mlstm(qk, z, gate_b[i], row(mlstm_norm[i]), batch, seq)
        h = _out_proj(h, (y_diff, y_hgrn, y_mla, y_mlstm), row(group_gain[i]), cast(w_out, 1024))
        pq, xn = _norm_mm(h, row(ln_ffn[i]), cast(peer_w_q, 1024), BF16, tm=512, tn=2048, emit_xn=True)
        yt = _peer_dense(xn.T, cast(peer_u, 1024), _cast_transpose(peer_v, i, BF16),
                         *_peer_router(pq, subkeys_b[i]))
        h = _ple(h, yt, row(ln_ple[i]), cast(ple_w_gate, 1024), p[i].reshape(t, PLE_DIM),
                 ple_w_proj[i].astype(BF16), row(ln_final), final=(i == depth - 1))
    return h.reshape(batch, seq, d)
```

```python
import functools
import math

import jax
import jax.numpy as jnp
from jax import lax
from jax.experimental import pallas as pl
from jax.experimental.pallas import tpu as pltpu

F32 = jnp.float32
BF16 = jnp.bfloat16

D_MODEL = 2048
PLE_DIM = 256
RMS_EPS = 1e-6
ROPE_THETA = 10000.0
HEADS = 4
HEAD_V = 128
MIX_W = HEADS * HEAD_V
DIFF_QK = 64
HGRN_DK = 128
MLA_NOPE = 128
MLA_ROPE = 64
MLA_Q_RANK = 384
MLA_KV_RANK = 256
MLA_QK_PAD = 256
MLSTM_DQK = 64
MLSTM_CONV = 4
CHUNK = 128
HGRN_CHUNK = 64
SUB = 16
PEER_HEADS = 8
PEER_KEYS = 128
PEER_KEY_DIM = 128
PEER_TOPK = 16
PEER_EXPERTS = PEER_KEYS * PEER_KEYS
NEG = -1e30

VMEM_LIMIT = 56 * 1024 * 1024
LANES = 128
SUBLANES = 8

IN_SIZES = (512, 512, 512, 512, 512, 512, 512, 384, 256, 64, 256, 256, 512, 4, 4, 512)
C_DQ, C_DK, C_DV, C_GQ, C_GF, C_GV, C_GG, C_LV, C_LO = (i * 512 for i in range(9))
C_CKV, C_LQ, C_LK = 4608, 4864, 5120
C_CQ = 5376
C_KR, C_KRR, C_GATE = 5760, 5888, 6016
N_IN = 6144


def _cparams(sem):
    return pltpu.CompilerParams(dimension_semantics=sem, vmem_limit_bytes=VMEM_LIMIT)


def _rms(x, g):
    return x * lax.rsqrt(jnp.mean(x * x, axis=-1, keepdims=True) + RMS_EPS) * g


def _dot(a, b):
    return jnp.dot(a, b, preferred_element_type=F32)


def _dot_nt(a, b):
    return lax.dot_general(a, b, (((1,), (1,)), ((), ())), preferred_element_type=F32)


def _dot_tn(a, b):
    return lax.dot_general(a, b, (((0,), (0,)), ((), ())), preferred_element_type=F32)


def _log_sigmoid(x):
    return jnp.minimum(x, 0.0) - jnp.log1p(jnp.exp(-jnp.abs(x)))


def _cumsum_rows(x):
    row = lax.broadcasted_iota(jnp.int32, x.shape, 0)
    k = 1
    while k < x.shape[0]:
        x = x + jnp.where(row >= k, pltpu.roll(x, k, 0), 0.0)
        k *= 2
    return x


def _norm_mm_body(x_ref, g_ref, w_ref, o_ref, *rest, emit_xn):
    xn_ref = rest[-1]

    @pl.when(pl.program_id(1) == 0)
    def _():
        y = _rms(x_ref[...], g_ref[...]).astype(BF16)
        xn_ref[...] = y
        if emit_xn:
            rest[0][...] = y

    o_ref[...] = _dot(xn_ref[...], w_ref[...]).astype(o_ref.dtype)


def _norm_mm(x, g, w, out_dtype, tm, tn, emit_xn=False):
    t, d = x.shape
    n = w.shape[1]
    out_shape = [jax.ShapeDtypeStruct((t, n), out_dtype)]
    out_specs = [pl.BlockSpec((tm, tn), lambda i, j: (i, j))]
    if emit_xn:
        out_shape.append(jax.ShapeDtypeStruct((t, d), BF16))
        out_specs.append(pl.BlockSpec((tm, d), lambda i, j: (i, 0)))
    res = pl.pallas_call(
        functools.partial(_norm_mm_body, emit_xn=emit_xn),
        grid=(t // tm, n // tn),
        in_specs=[pl.BlockSpec((tm, d), lambda i, j: (i, 0)),
                  pl.BlockSpec((1, d), lambda i, j: (0, 0)),
                  pl.BlockSpec((d, tn), lambda i, j: (0, j))],
        out_specs=out_specs,
        out_shape=out_shape,
        scratch_shapes=[pltpu.VMEM((tm, d), BF16)],
        compiler_params=_cparams(("parallel", "arbitrary")),
        name="norm_mm",
    )(x, g, w)
    return res if emit_xn else res[0]


def _causal_mask(i, tq):
    row = i * tq + lax.broadcasted_iota(jnp.int32, (tq, (i + 1) * tq), 0)
    col = lax.broadcasted_iota(jnp.int32, (tq, (i + 1) * tq), 1)
    return col <= row


def _softmax_terms(s, mask):
    s = jnp.where(mask, s, NEG)
    e = jnp.exp(s - jnp.max(s, axis=-1, keepdims=True))
    return e, 1.0 / jnp.sum(e, axis=-1, keepdims=True)


def _per_query_block(nq, fn):
    for i in range(nq):
        pl.when(pl.program_id(2) == i)(functools.partial(fn, i))


def _diff_attn_body(lam_ref, q_ref, k_ref, v_ref, g_ref, o_ref, *, tq, nq, lam_init):
    def block(i):
        kv = (i + 1) * tq
        mask = _causal_mask(i, tq)
        q = q_ref[...] * (DIFF_QK ** -0.5)
        lane = lax.broadcasted_iota(jnp.int32, q.shape, 1)
        k = k_ref[0:kv, :].astype(BF16)
        e1, r1 = _softmax_terms(_dot_nt(jnp.where(lane < DIFF_QK, q, 0.0).astype(BF16), k), mask)
        e2, r2 = _softmax_terms(_dot_nt(jnp.where(lane >= DIFF_QK, q, 0.0).astype(BF16), k), mask)
        a = e1 * r1 - e2 * (lam_ref[0, 0] * r2)
        o = _dot(a.astype(BF16), v_ref[0:kv, :].astype(BF16))
        o_ref[...] = _rms(o, g_ref[...]) * (1.0 - lam_init)

    _per_query_block(nq, block)


def _diff_attn(z, lam, g, batch, seq, lam_init, tq=256):
    nq = seq // tq
    return pl.pallas_call(
        functools.partial(_diff_attn_body, tq=tq, nq=nq, lam_init=lam_init),
        grid=(batch, HEADS, nq),
        in_specs=[pl.BlockSpec(memory_space=pltpu.SMEM),
                  pl.BlockSpec((tq, LANES), lambda b, h, i: (b * nq + i, C_DQ // LANES + h)),
                  pl.BlockSpec((seq, LANES), lambda b, h, i: (b, C_DK // LANES + h)),
                  pl.BlockSpec((seq, LANES), lambda b, h, i: (b, C_DV // LANES + h)),
                  pl.BlockSpec((1, LANES), lambda b, h, i: (0, h))],
        out_specs=pl.BlockSpec((tq, LANES), lambda b, h, i: (b * nq + i, h)),
        out_shape=jax.ShapeDtypeStruct((batch * seq, MIX_W), F32),
        compiler_params=_cparams(("parallel", "parallel", "arbitrary")),
        name="diff_attn",
    )(lam, z, z, z, g)


def _mla_proj_body(cq_ref, ckv_ref, kr_ref, krr_ref, pos_ref, inv_ref, qn_ref, kvn_ref,
                   wq_ref, wkv_ref, q_ref, k_ref, v_ref):
    qa = _dot(_rms(cq_ref[...], qn_ref[...]).astype(BF16), wq_ref[...])
    kva = _dot(_rms(ckv_ref[...], kvn_ref[...]).astype(BF16), wkv_ref[...])
    ang = pos_ref[...] * inv_ref[...]
    cos, sin = jnp.cos(ang), jnp.sin(ang)
    k_rope = (kr_ref[...] * cos + krr_ref[...] * sin).astype(BF16)
    scale = (MLA_NOPE + MLA_ROPE) ** -0.5
    for h in range(HEADS):
        lo, hi = h * LANES, (h + 1) * LANES
        q_rope = qa[:, MIX_W + lo:MIX_W + hi] * cos + qa[:, 2 * MIX_W + lo:2 * MIX_W + hi] * sin
        q_ref[:, 2 * lo:2 * lo + LANES] = (qa[:, lo:hi] * scale).astype(BF16)
        q_ref[:, 2 * lo + LANES:2 * hi] = (q_rope * scale).astype(BF16)
        k_ref[:, 2 * lo:2 * lo + LANES] = kva[:, lo:hi].astype(BF16)
        k_ref[:, 2 * lo + LANES:2 * hi] = k_rope
    v_ref[...] = kva[:, MIX_W:].astype(BF16)


def _mla_proj(z, pos, inv, qn, kvn, wq, wkv, tm=512):
    t = z.shape[0]
    row = lambda c, w: pl.BlockSpec((tm, w), lambda i: (i, c // w))
    full = lambda a: pl.BlockSpec(a.shape, lambda i: (0, 0))
    return pl.pallas_call(
        _mla_proj_body,
        grid=(t // tm,),
        in_specs=[row(C_CQ, MLA_Q_RANK), row(C_CKV, MLA_KV_RANK), row(C_KR, LANES), row(C_KRR, LANES),
                  pl.BlockSpec((tm, 1), lambda i: (i, 0)), full(inv), full(qn), full(kvn), full(wq), full(wkv)],
        out_specs=[pl.BlockSpec((tm, HEADS * MLA_QK_PAD), lambda i: (i, 0)),
                   pl.BlockSpec((tm, HEADS * MLA_QK_PAD), lambda i: (i, 0)),
                   pl.BlockSpec((tm, MIX_W), lambda i: (i, 0))],
        out_shape=[jax.ShapeDtypeStruct((t, HEADS * MLA_QK_PAD), BF16),
                   jax.ShapeDtypeStruct((t, HEADS * MLA_QK_PAD), BF16),
                   jax.ShapeDtypeStruct((t, MIX_W), BF16)],
        compiler_params=_cparams(("parallel",)),
        name="mla_proj",
    )(z, z, z, z, pos, inv, qn, kvn, wq, wkv)


def _mla_attn_body(q_ref, k_ref, v_ref, o_ref, *, tq, nq):
    def block(i):
        kv = (i + 1) * tq
        e, r = _softmax_terms(_dot_nt(q_ref[...], k_ref[0:kv, :]), _causal_mask(i, tq))
        o_ref[...] = _dot((e * r).astype(BF16), v_ref[0:kv, :])

    _per_query_block(nq, block)


def _mla_attn(q, k, v, batch, seq, tq=256):
    nq = seq // tq
    return pl.pallas_call(
        functools.partial(_mla_attn_body, tq=tq, nq=nq),
        grid=(batch, HEADS, nq),
        in_specs=[pl.BlockSpec((tq, MLA_QK_PAD), lambda b, h, i: (b * nq + i, h)),
                  pl.BlockSpec((seq, MLA_QK_PAD), lambda b, h, i: (b, h)),
                  pl.BlockSpec((seq, LANES), lambda b, h, i: (b, h))],
        out_specs=pl.BlockSpec((tq, LANES), lambda b, h, i: (b * nq + i, h)),
        out_shape=jax.ShapeDtypeStruct((batch * seq, MIX_W), F32),
        compiler_params=_cparams(("parallel", "parallel", "arbitrary")),
        name="mla_attn",
    )(q, k, v)


def _hgrn_body(q_ref, f_ref, v_ref, g_ref, par_ref, o_ref, st_ref):
    @pl.when(pl.program_id(1) == 0)
    def _():
        st_ref[...] = jnp.zeros_like(st_ref)

    sub_row = lax.broadcasted_iota(jnp.int32, (SUB, SUB), 0)
    sub_col = lax.broadcasted_iota(jnp.int32, (SUB, SUB), 1)
    for h in range(HEADS):
        sl = slice(h * LANES, (h + 1) * LANES)
        log_lb, log_1m_lb, one_m_lb, gain = (par_ref[r:r + 1, sl] for r in range(4))
        zf = f_ref[:, sl]
        q = q_ref[:, sl] * (HGRN_DK ** -0.5)
        v = v_ref[:, sl].astype(BF16)
        k = one_m_lb * jax.nn.sigmoid(-zf)
        lf_a, lf_b = log_lb, log_1m_lb + _log_sigmoid(zf)
        log_f = jnp.maximum(lf_a, lf_b) + jnp.log1p(jnp.exp(-jnp.abs(lf_a - lf_b)))
        b = _cumsum_rows(log_f)
        st = st_ref[h]
        o = _dot_nt((q * jnp.exp(b)).astype(BF16), st.astype(BF16))
        parts = []
        for i in range(HGRN_CHUNK // SUB):
            lo = i * SUB
            qi, bi, ki, vi = q[lo:lo + SUB], b[lo:lo + SUB], k[lo:lo + SUB], v[lo:lo + SUB]
            sc = jnp.zeros((SUB, SUB), F32)
            for s in range(SUB):
                dec = jnp.exp(jnp.minimum(bi - bi[s:s + 1], 0.0))
                col = jnp.sum(qi * ki[s:s + 1] * dec, axis=-1, keepdims=True)
                sc = sc + jnp.where(sub_col == s, col, 0.0)
            sc = jnp.where(sub_col <= sub_row, sc, 0.0)
            part = _dot(sc.astype(BF16), vi)
            if i > 0:
                b0 = b[lo - 1:lo]
                qe = (qi * jnp.exp(bi - b0)).astype(BF16)
                ke = (k[:lo] * jnp.exp(b0 - b[:lo])).astype(BF16)
                part = part + _dot(_dot_nt(qe, ke).astype(BF16), v[:lo])
            parts.append(part)
        o = o + jnp.concatenate(parts, axis=0)
        b_last = b[HGRN_CHUNK - 1:HGRN_CHUNK]
        st_ref[h] = st * jnp.exp(b_last) + _dot_tn(v, (k * jnp.exp(b_last - b)).astype(BF16))
        o_ref[:, sl] = _rms(o, gain) * jax.nn.silu(g_ref[:, sl])


def _hgrn(z, par, batch, seq):
    nc = seq // HGRN_CHUNK
    blk = lambda c: pl.BlockSpec((HGRN_CHUNK, MIX_W), lambda b, i: (b * nc + i, c // MIX_W))
    return pl.pallas_call(
        _hgrn_body,
        grid=(batch, nc),
        in_specs=[blk(C_GQ), blk(C_GF), blk(C_GV), blk(C_GG), pl.BlockSpec((8, MIX_W), lambda b, i: (0, 0))],
        out_specs=pl.BlockSpec((HGRN_CHUNK, MIX_W), lambda b, i: (b * nc + i, 0)),
        out_shape=jax.ShapeDtypeStruct((batch * seq, MIX_W), F32),
        scratch_shapes=[pltpu.VMEM((HEADS, HEAD_V, HGRN_DK), F32)],
        compiler_params=_cparams(("parallel", "arbitrary")),
        name="hgrn2",
    )(z, z, z, z, par)


def _conv_silu_body(x_ref, w_ref, b_ref, o_ref):
    x = x_ref[...]
    row = lax.broadcasted_iota(jnp.int32, x.shape, 0)
    y = x * w_ref[MLSTM_CONV - 1:MLSTM_CONV] + b_ref[...]
    for k in range(1, MLSTM_CONV):
        y = y + jnp.where(row >= k, pltpu.roll(x, k, 0), 0.0) * w_ref[MLSTM_CONV - 1 - k:MLSTM_CONV - k]
    o_ref[...] = jax.nn.silu(y)


def _conv_silu(z, w, b, batch, seq):
    wd = HEADS * MLSTM_DQK
    return pl.pallas_call(
        _conv_silu_body,
        grid=(batch, 2),
        in_specs=[pl.BlockSpec((seq, wd), lambda bi, j: (bi, C_LQ // wd + j)),
                  pl.BlockSpec((MLSTM_CONV, wd), lambda bi, j: (0, j)),
                  pl.BlockSpec((1, wd), lambda bi, j: (0, j))],
        out_specs=pl.BlockSpec((seq, wd), lambda bi, j: (bi, j)),
        out_shape=jax.ShapeDtypeStruct((batch * seq, 2 * wd), F32),
        compiler_params=_cparams(("parallel", "parallel")),
        name="conv_silu",
    )(z, w, b)


def _mlstm_body(qk_ref, v_ref, gate_ref, og_ref, gb_ref, norm_ref, o_ref, c_ref, m_ref):
    @pl.when(pl.program_id(1) == 0)
    def _():
        c_ref[...] = jnp.zeros_like(c_ref)
        m_ref[...] = jnp.zeros_like(m_ref)

    g = gate_ref[...] + gb_ref[...]
    lane = lax.broadcasted_iota(jnp.int32, g.shape, 1)
    gt = jnp.where(lane < HEADS, g, _cumsum_rows(_log_sigmoid(g)))
    gt_t = gt.T
    row = lax.broadcasted_iota(jnp.int32, (CHUNK, CHUNK), 0)
    col = lax.broadcasted_iota(jnp.int32, (CHUNK, CHUNK), 1)
    ones_col = jnp.where(lane == 0, 1.0, 0.0).astype(BF16)
    half_row = lax.broadcasted_iota(jnp.int32, (LANES, 1), 0) < MLSTM_DQK
    for p in range(HEADS // 2):
        q_t = qk_ref[:, p * LANES:(p + 1) * LANES] * (MLSTM_DQK ** -0.5)
        k_t = qk_ref[:, HEADS * MLSTM_DQK + p * LANES:HEADS * MLSTM_DQK + (p + 1) * LANES]
        c_old = c_ref[p]
        c_bf = c_old.astype(BF16)
        upd = jnp.zeros_like(c_old)
        carries = []
        for u in range(2):
            h = 2 * p + u
            mine = (lane >= u * MLSTM_DQK) & (lane < (u + 1) * MLSTM_DQK)
            li_col, b_col = gt[:, h:h + 1], gt[:, HEADS + h:HEADS + h + 1]
            li_row, b_row = gt_t[h:h + 1, :], gt_t[HEADS + h:HEADS + h + 1, :]
            m_st = m_ref[h:h + 1, 0:1]
            dmat = jnp.where(col <= row, b_col - b_row + li_row, NEG)
            inter_log = b_col + m_st
            m_t = jnp.maximum(inter_log, jnp.max(dmat, axis=-1, keepdims=True))
            qm = jnp.where(mine, q_t, 0.0).astype(BF16)
            s = _dot_nt(qm, k_t.astype(BF16)) * jnp.exp(dmat - m_t)
            v_aug = jnp.concatenate([v_ref[:, h * LANES:(h + 1) * LANES].astype(BF16), ones_col], axis=1)
            out = jnp.exp(inter_log - m_t) * _dot(qm, c_bf) + _dot(s.astype(BF16), v_aug)
            hid = out[:, :HEAD_V] / jnp.maximum(jnp.abs(out[:, HEAD_V:HEAD_V + 1]), jnp.exp(-m_t))
            sl = slice(h * LANES, (h + 1) * LANES)
            o_ref[:, sl] = _rms(hid, norm_ref[:, sl]) * jax.nn.sigmoid(og_ref[:, sl])
            b_last = b_col[CHUNK - 1:CHUNK]
            log_s = b_last - b_col + li_col
            m_new = jnp.maximum(b_last + m_st, jnp.max(log_s, axis=0, keepdims=True))
            carries.append(jnp.exp(b_last + m_st - m_new))
            kw = (jnp.where(mine, k_t, 0.0) * jnp.exp(log_s - m_new)).astype(BF16)
            upd = upd + _dot_tn(kw, v_aug)
            m_ref[h:h + 1, :] = jnp.broadcast_to(m_new, (1, LANES))
        c_ref[p] = jnp.where(half_row, carries[0], carries[1]) * c_old + upd


def _mlstm(qk, z, gb, norm, batch, seq):
    nc = seq // CHUNK
    return pl.pallas_call(
        _mlstm_body,
        grid=(batch, nc),
        in_specs=[pl.BlockSpec((CHUNK, 2 * HEADS * MLSTM_DQK), lambda b, i: (b * nc + i, 0)),
                  pl.BlockSpec((CHUNK, MIX_W), lambda b, i: (b * nc + i, C_LV // MIX_W)),
                  pl.BlockSpec((CHUNK, LANES), lambda b, i: (b * nc + i, C_GATE // LANES)),
                  pl.BlockSpec((CHUNK, MIX_W), lambda b, i: (b * nc + i, C_LO // MIX_W)),
                  pl.BlockSpec((1, LANES), lambda b, i: (0, 0)),
                  pl.BlockSpec((1, MIX_W), lambda b, i: (0, 0))],
        out_specs=pl.BlockSpec((CHUNK, MIX_W), lambda b, i: (b * nc + i, 0)),
        out_shape=jax.ShapeDtypeStruct((batch * seq, MIX_W), F32),
        scratch_shapes=[pltpu.VMEM((HEADS // 2, LANES, 2 * HEAD_V), F32), pltpu.VMEM((8, LANES), F32)],
        compiler_params=_cparams(("parallel", "arbitrary")),
        name="mlstm",
    )(qk, z, z, z, gb, norm)


def _out_proj_body(h_ref, y0_ref, y1_ref, y2_ref, y3_ref, g_ref, w_ref, o_ref):
    acc = h_ref[...]
    for i, y_ref in enumerate((y0_ref, y1_ref, y2_ref, y3_ref)):
        sl = slice(i * MIX_W, (i + 1) * MIX_W)
        acc = acc + _dot((y_ref[...] * g_ref[:, sl]).astype(BF16), w_ref[sl, :])
    o_ref[...] = acc


def _out_proj(h, ys, g, w, tm=512):
    t, d = h.shape
    yspec = pl.BlockSpec((tm, MIX_W), lambda i: (i, 0))
    return pl.pallas_call(
        _out_proj_body,
        grid=(t // tm,),
        in_specs=[pl.BlockSpec((tm, d), lambda i: (i, 0)), yspec, yspec, yspec, yspec,
                  pl.BlockSpec((1, HEADS * MIX_W), lambda i: (0, 0)),
                  pl.BlockSpec((HEADS * MIX_W, d), lambda i: (0, 0))],
        out_specs=pl.BlockSpec((tm, d), lambda i: (i, 0)),
        out_shape=jax.ShapeDtypeStruct((t, d), F32),
        compiler_params=_cparams(("parallel",)),
        name="out_proj",
    )(h, *ys, g, w)


def _ranked_top(x, n, dst_ref):
    rank = jnp.full(x.shape, float(n), F32)
    for r in range(n):
        mx = jnp.max(x, axis=0, keepdims=True)
        dst_ref[r:r + 1, :] = mx
        eq = x == mx
        rank = jnp.where(eq, float(r), rank)
        x = jnp.where(eq, NEG, x)
    return rank


def _kth_and_next(x, k):
    cum = jnp.zeros((1, x.shape[1]), F32)
    kth = jnp.full((1, x.shape[1]), NEG, F32)
    nxt = kth
    for _ in range(k + 1):
        mx = jnp.max(x, axis=0, keepdims=True)
        eq = x == mx
        new = cum + jnp.sum(eq.astype(F32), axis=0, keepdims=True)
        kth = jnp.where((cum < k) & (new >= k), mx, kth)
        nxt = jnp.where((cum < k + 1) & (new >= k + 1), mx, nxt)
        cum = new
        x = jnp.where(eq, NEG, x)
    return kth, nxt


PAIR_COLS = tuple((PEER_TOPK + 1) // (i + 1) for i in range(PEER_TOPK + 1))
PAIR_OFFS = tuple(sum(PAIR_COLS[:i]) for i in range(PEER_TOPK + 2))
PAIR_ROWS = -(-PAIR_OFFS[-1] // SUBLANES) * SUBLANES


def _pack_halves(x):
    bits = pltpu.bitcast(x.astype(BF16).astype(F32), jnp.uint32)
    half = x.shape[0] // 2
    return (bits[half:] & jnp.uint32(0xFFFF0000)) | (bits[:half] >> 16)


def _peer_router_body(q_ref, sk_ref, nb_ref, e1_ref, rb_ref, e2_ref, a_ref, b_ref, g_ref):
    tt = q_ref.shape[0]
    for h in range(PEER_HEADS):
        s1 = _dot_nt(sk_ref[2 * h], q_ref[:, 2 * h * LANES:(2 * h + 1) * LANES])
        s2 = _dot_nt(sk_ref[2 * h + 1], q_ref[:, (2 * h + 1) * LANES:(2 * h + 2) * LANES])
        rank_a = _ranked_top(s1, PEER_TOPK + 1, a_ref)
        rank_b = _ranked_top(s2, PEER_TOPK + 1, b_ref)
        g_ref[PAIR_ROWS - SUBLANES:PAIR_ROWS, :] = jnp.full((SUBLANES, tt), NEG, F32)
        for i, n in enumerate(PAIR_COLS):
            g_ref[PAIR_OFFS[i]:PAIR_OFFS[i] + n, :] = a_ref[i:i + 1, :] + b_ref[0:n, :]
        pair_sums = g_ref[...]
        kth, nxt = _kth_and_next(pair_sums, PEER_TOPK)
        thr = 0.5 * (kth + nxt)
        a_top, b_top = a_ref[0:1, :], b_ref[0:1, :]
        chosen = pair_sums >= thr
        zsum = jnp.sum(jnp.where(chosen, jnp.exp(pair_sums - (a_top + b_top)), 0.0), axis=0, keepdims=True)
        ones = jnp.where(chosen, 1.0, 0.0)
        count_a = jnp.zeros_like(s1)
        for i in range(PEER_TOPK):
            n_i = jnp.sum(ones[PAIR_OFFS[i]:PAIR_OFFS[i + 1]], axis=0, keepdims=True)
            count_a = jnp.where(rank_a == float(i), n_i, count_a)
        e1 = jnp.exp(s1 - a_top).astype(BF16).astype(F32)
        for q in range(tt // LANES):
            nb_ref[q, h] = count_a[:, q * LANES:(q + 1) * LANES]
            e1_ref[q, h] = e1[:, q * LANES:(q + 1) * LANES]
        rb_ref[h] = _pack_halves(rank_b)
        e2_ref[h] = _pack_halves(jnp.exp(s2 - b_top) * (0.5 / zsum))


def _peer_router(q, sk, tt=256):
    t = q.shape[0]
    row_spec = pl.BlockSpec((tt // LANES, PEER_HEADS, PEER_KEYS, LANES), lambda i: (i, 0, 0, 0))
    row_shape = jax.ShapeDtypeStruct((t // LANES, PEER_HEADS, PEER_KEYS, LANES), F32)
    key_spec = pl.BlockSpec((PEER_HEADS, PEER_KEYS // 2, tt), lambda i: (0, 0, i))
    key_shape = jax.ShapeDtypeStruct((PEER_HEADS, PEER_KEYS // 2, t), jnp.uint32)
    return pl.pallas_call(
        _peer_router_body,
        grid=(t // tt,),
        in_specs=[pl.BlockSpec((tt, q.shape[1]), lambda i: (i, 0)),
                  pl.BlockSpec(sk.shape, lambda i: (0, 0, 0))],
        out_specs=[row_spec, row_spec, key_spec, key_spec],
        out_shape=[row_shape, row_shape, key_shape, key_shape],
        scratch_shapes=[pltpu.VMEM((24, tt), F32), pltpu.VMEM((24, tt), F32),
                        pltpu.VMEM((PAIR_ROWS, tt), F32)],
        compiler_params=_cparams(("parallel",)),
        name="peer_router",
    )(q, sk)


PEER_MROWS = 256


def _peer_dense_body(xt_ref, u_ref, vt_ref, nb_ref, e1_ref, rb_ref, e2_ref, o_ref, act_ref, wa_ref, *, rows):
    @pl.when(pl.program_id(1) == 0)
    def _():
        o_ref[...] = jnp.zeros_like(o_ref)

    et, tt = act_ref.shape
    half = PEER_KEYS // 2
    for m in range(et // PEER_MROWS):
        ms = slice(m * PEER_MROWS, (m + 1) * PEER_MROWS)
        act_ref[ms, :] = _dot(u_ref[ms, :], xt_ref[...])

    def gelu2(z):
        return z + z * lax.erf(z * (2.0 ** -0.5))

    for a in range(rows):
        for q in range(tt // LANES):
            tok = slice(q * LANES, (q + 1) * LANES)

            def row(ref, h):
                bits = pltpu.bitcast(jnp.broadcast_to(ref[q, h, a:a + 1, :], (SUBLANES, LANES)), jnp.uint32)
                return pltpu.bitcast((bits & jnp.uint32(0xFFFF0000)) | (bits >> 16), BF16)

            nbs = [row(nb_ref, h) for h in range(PEER_HEADS)]
            e1s = [row(e1_ref, h) for h in range(PEER_HEADS)]
            for s in range(half // SUBLANES):
                ks = slice(s * SUBLANES, (s + 1) * SUBLANES)
                w = None
                for h in range(PEER_HEADS):
                    rb = pltpu.bitcast(rb_ref[h, ks, tok], BF16)
                    e2 = pltpu.bitcast(e2_ref[h, ks, tok], BF16)
                    t = jnp.where(rb < nbs[h], e2 * e1s[h], 0.0)
                    w = t if w is None else w + t
                wbits = pltpu.bitcast(w, jnp.uint32)
                lo = slice(a * PEER_KEYS + s * SUBLANES, a * PEER_KEYS + (s + 1) * SUBLANES)
                hi = slice(a * PEER_KEYS + half + s * SUBLANES, a * PEER_KEYS + half + (s + 1) * SUBLANES)
                wa_ref[lo, tok] = (pltpu.bitcast(wbits << 16, F32) * gelu2(act_ref[lo, tok])).astype(BF16)
                wa_ref[hi, tok] = (pltpu.bitcast(wbits & jnp.uint32(0xFFFF0000), F32)
                                   * gelu2(act_ref[hi, tok])).astype(BF16)
    for m in range(o_ref.shape[0] // PEER_MROWS):
        ms = slice(m * PEER_MROWS, (m + 1) * PEER_MROWS)
        o_ref[ms, :] += _dot(vt_ref[ms, :], wa_ref[...])


def _peer_dense(xt, u, vt, nb, e1, rb, e2, tt=512, rows=8):
    d, t = xt.shape
    et = rows * PEER_KEYS
    tok_blk = pl.BlockSpec((d, tt), lambda i, j: (0, i))
    row_blk = pl.BlockSpec((tt // LANES, PEER_HEADS, rows, LANES), lambda i, j: (i, 0, j, 0))
    key_blk = pl.BlockSpec((PEER_HEADS, PEER_KEYS // 2, tt), lambda i, j: (0, 0, i))
    return pl.pallas_call(
        functools.partial(_peer_dense_body, rows=rows),
        grid=(t // tt, PEER_EXPERTS // et),
        in_specs=[tok_blk, pl.BlockSpec((et, d), lambda i, j: (j, 0)), pl.BlockSpec((d, et), lambda i, j: (0, j)),
                  row_blk, row_blk, key_blk, key_blk],
        out_specs=tok_blk,
        out_shape=jax.ShapeDtypeStruct((d, t), F32),
        scratch_shapes=[pltpu.VMEM((et, tt), F32), pltpu.VMEM((et, tt), BF16)],
        compiler_params=_cparams(("parallel", "arbitrary")),
        name="peer_dense",
    )(xt, u, vt, nb, e1, rb, e2)


def _ple_body(h_ref, yt_ref, g_ref, wg_ref, p_ref, wp_ref, fg_ref, o_ref, *, final):
    h = h_ref[...] + yt_ref[...].T
    gate = jax.nn.sigmoid(_dot(_rms(h, g_ref[...]).astype(BF16), wg_ref[...]))
    out = h + gate * _dot(p_ref[...].astype(BF16), wp_ref[...])
    o_ref[...] = _rms(out, fg_ref[...]) if final else out


def _ple(h, yt, g, wg, p, wp, fg, final, tm=512):
    t, d = h.shape
    full = lambda a: pl.BlockSpec(a.shape, lambda i: (0, 0))
    return pl.pallas_call(
        functools.partial(_ple_body, final=final),
        grid=(t // tm,),
        in_specs=[pl.BlockSpec((tm, d), lambda i: (i, 0)), pl.BlockSpec((d, tm), lambda i: (0, i)), full(g), full(wg),
                  pl.BlockSpec((tm, p.shape[1]), lambda i: (i, 0)), full(wp), full(fg)],
        out_specs=pl.BlockSpec((tm, d), lambda i: (i, 0)),
        out_shape=jax.ShapeDtypeStruct((t, d), F32),
        compiler_params=_cparams(("parallel",)),
        name="ple",
    )(h, yt, g, wg, p, wp, fg)


def _cast_body(x_ref, o_ref):
    o_ref[...] = x_ref[...].astype(o_ref.dtype)


def _cast_layer(x, layer, dtype, rows):
    _, r, c = x.shape
    return pl.pallas_call(
        _cast_body,
        grid=(r // rows,),
        in_specs=[pl.BlockSpec((None, rows, c), lambda i: (layer, i, 0))],
        out_specs=pl.BlockSpec((rows, c), lambda i: (i, 0)),
        out_shape=jax.ShapeDtypeStruct((r, c), dtype),
        compiler_params=_cparams(("parallel",)),
        name="cast_layer",
    )(x)


def _cast_transpose_body(x_ref, o_ref):
    o_ref[...] = x_ref[...].T.astype(o_ref.dtype)


def _cast_transpose(x, layer, dtype, tile=1024):
    _, r, c = x.shape
    return pl.pallas_call(
        _cast_transpose_body,
        grid=(r // tile, c // tile),
        in_specs=[pl.BlockSpec((None, tile, tile), lambda i, j: (layer, i, j))],
        out_specs=pl.BlockSpec((tile, tile), lambda i, j: (j, i)),
        out_shape=jax.ShapeDtypeStruct((c, r), dtype),
        compiler_params=_cparams(("parallel", "parallel")),
        name="cast_transpose",
    )(x)


def _rot_half_cols(w):
    half = w.shape[-1] // 2
    return jnp.concatenate([-w[..., half:], w[..., :half]], axis=-1)


def _pad_cols(w, width):
    return jnp.pad(w, [(0, 0)] * (w.ndim - 1) + [(0, width - w.shape[-1])])


def _prep_w_in(w_in):
    offs = [0]
    for s in IN_SIZES:
        offs.append(offs[-1] + s)
    (d_q, d_k, d_v, g_q, g_f, g_v, g_g, m_cq, m_ckv, m_kr,
     l_q, l_k, l_v, l_i, l_f, l_o) = (w_in[..., offs[i]:offs[i + 1]] for i in range(len(IN_SIZES)))
    pieces = [d_q, d_k, d_v, g_q, g_f, g_v, g_g, l_v, l_o, m_ckv, l_q, l_k, m_cq,
              _pad_cols(m_kr, LANES), _pad_cols(_rot_half_cols(m_kr), LANES),
              _pad_cols(jnp.concatenate([l_i, l_f], axis=-1), LANES)]
    return jnp.concatenate(pieces, axis=-1)


def _prep_mla(w_uq, w_ukv):
    nl = w_uq.shape[0]
    wq = w_uq.reshape(nl, MLA_Q_RANK, HEADS, MLA_NOPE + MLA_ROPE)
    rope = wq[..., MLA_NOPE:]
    wq_all = jnp.concatenate([
        wq[..., :MLA_NOPE].reshape(nl, MLA_Q_RANK, MIX_W),
        _pad_cols(rope, LANES).reshape(nl, MLA_Q_RANK, MIX_W),
        _pad_cols(_rot_half_cols(rope), LANES).reshape(nl, MLA_Q_RANK, MIX_W)], axis=-1)
    wkv = w_ukv.reshape(nl, MLA_KV_RANK, HEADS, MLA_NOPE + HEAD_V)
    wkv_all = jnp.concatenate([wkv[..., :MLA_NOPE].reshape(nl, MLA_KV_RANK, MIX_W),
                               wkv[..., MLA_NOPE:].reshape(nl, MLA_KV_RANK, MIX_W)], axis=-1)
    return wq_all.astype(BF16), wkv_all.astype(BF16)


def kernel(x, p, positions, ln_mix, w_in, diff_lambda, diff_norm, hgrn_lb_logits, hgrn_norm, mla_q_norm,
           mla_kv_norm, mla_w_uq, mla_w_ukv, mlstm_conv_w, mlstm_conv_b, mlstm_gate_b, mlstm_norm, group_gain,
           w_out, ln_ffn, peer_w_q, peer_subkeys, peer_u, peer_v, ln_ple, ple_w_gate, ple_w_proj, ln_final):
    batch, seq, d = x.shape
    depth = w_in.shape[0]
    t = batch * seq

    wq_all, wkv_all = _prep_mla(mla_w_uq, mla_w_ukv)
    subkeys_b = peer_subkeys.reshape(depth, PEER_HEADS * 2, PEER_KEYS, PEER_KEY_DIM).astype(BF16)

    lv = diff_lambda.astype(F32)
    lam = jnp.exp(jnp.sum(lv[:, 0] * lv[:, 1], axis=-1)) - jnp.exp(jnp.sum(lv[:, 2] * lv[:, 3], axis=-1))
    lbs = jnp.cumsum(jax.nn.softmax(hgrn_lb_logits.astype(F32), axis=0), axis=0)
    lbs = lbs - lbs[:1]
    hgrn_par = jnp.stack([jnp.maximum(jnp.log(lbs), NEG), jnp.log1p(-lbs), 1.0 - lbs, hgrn_norm], axis=1)
    hgrn_par = jnp.pad(hgrn_par, ((0, 0), (0, 4), (0, 0)))
    gate_b = _pad_cols(mlstm_gate_b.reshape(depth, 1, 2 * HEADS), LANES)
    half = MLA_ROPE // 2
    inv = ROPE_THETA ** (-jnp.arange(half, dtype=F32) / half)
    inv = _pad_cols(jnp.concatenate([inv, inv])[None, :], LANES)
    pos = positions.astype(F32).reshape(t, 1)
    row = lambda a: a.reshape(1, -1)

    h = x.reshape(t, d)
    for i in range(depth):
        lam_init = 0.8 - 0.6 * math.exp(-0.3 * i)
        cast = lambda a, rows: _cast_layer(a, i, BF16, rows)
        z = _norm_mm(h, row(ln_mix[i]), _prep_w_in(w_in[i].astype(BF16)), F32, tm=512, tn=2048)
        y_diff = _diff_attn(z, (lam[i] + lam_init).reshape(1, 1), row(diff_norm[i]), batch, seq, lam_init)
        y_hgrn = _hgrn(z, hgrn_par[i], batch, seq)
        mq, mk, mv = _mla_proj(z, pos, inv, row(mla_q_norm[i]), row(mla_kv_norm[i]), wq_all[i], wkv_all[i])
        y_mla = _mla_attn(mq, mk, mv, batch, seq)
        qk = _conv_silu(z, mlstm_conv_w[i], row(mlstm_conv_b[i]), batch, seq)
        y_mlstm = _mlstm(qk, z, gate_b[i], row(mlstm_norm[i]), batch, seq)
        h = _out_proj(h, (y_diff, y_hgrn, y_mla, y_mlstm), row(group_gain[i]), cast(w_out, 1024))
        pq, xn = _norm_mm(h, row(ln_ffn[i]), cast(peer_w_q, 1024), BF16, tm=512, tn=2048, emit_xn=True)
        yt = _peer_dense(xn.T, cast(peer_u, 2048), _cast_transpose(peer_v, i, BF16),
                         *_peer_router(pq, subkeys_b[i]))
        h = _ple(h, yt, row(ln_ple[i]), cast(ple_w_gate, 1024), p[i].reshape(t, PLE_DIM),
                 ple_w_proj[i].astype(BF16), row(ln_final), final=(i == depth - 1))
    return h.reshape(batch, seq, d)
```

```python
import functools
import math

import jax
import jax.numpy as jnp
from jax import lax
from jax.experimental import pallas as pl
from jax.experimental.pallas import tpu as pltpu

F32 = jnp.float32
BF16 = jnp.bfloat16

D_MODEL = 2048
PLE_DIM = 256
RMS_EPS = 1e-6
ROPE_THETA = 10000.0
HEADS = 4
HEAD_V = 128
MIX_W = HEADS * HEAD_V
DIFF_QK = 64
HGRN_DK = 128
MLA_NOPE = 128
MLA_ROPE = 64
MLA_Q_RANK = 384
MLA_KV_RANK = 256
MLA_QK_PAD = 256
MLSTM_DQK = 64
MLSTM_CONV = 4
CHUNK = 256
HGRN_CHUNK = 64
SUB = 16
PEER_HEADS = 8
PEER_KEYS = 128
PEER_KEY_DIM = 128
PEER_TOPK = 16
PEER_EXPERTS = PEER_KEYS * PEER_KEYS
NEG = -1e30

VMEM_LIMIT = 56 * 1024 * 1024
LANES = 128
SUBLANES = 8

IN_SIZES = (512, 512, 512, 512, 512, 512, 512, 384, 256, 64, 256, 256, 512, 4, 4, 512)
C_DQ, C_DK, C_DV, C_GQ, C_GF, C_GV, C_GG, C_LV, C_LO = (i * 512 for i in range(9))
C_CKV, C_LQ, C_LK = 4608, 4864, 5120
C_CQ = 5376
C_KR, C_KRR, C_GATE = 5760, 5888, 6016
N_IN = 6144


def _cparams(sem):
    return pltpu.CompilerParams(dimension_semantics=sem, vmem_limit_bytes=VMEM_LIMIT)


def _rms(x, g):
    return x * lax.rsqrt(jnp.mean(x * x, axis=-1, keepdims=True) + RMS_EPS) * g


def _dot(a, b):
    return jnp.dot(a, b, preferred_element_type=F32)


def _dot_nt(a, b):
    return lax.dot_general(a, b, (((1,), (1,)), ((), ())), preferred_element_type=F32)


def _dot_tn(a, b):
    return lax.dot_general(a, b, (((0,), (0,)), ((), ())), preferred_element_type=F32)


def _log_sigmoid(x):
    return jnp.minimum(x, 0.0) - jnp.log1p(jnp.exp(-jnp.abs(x)))


def _cumsum_rows(x):
    row = lax.broadcasted_iota(jnp.int32, x.shape, 0)
    k = 1
    while k < x.shape[0]:
        x = x + jnp.where(row >= k, pltpu.roll(x, k, 0), 0.0)
        k *= 2
    return x


def _norm_mm_body(x_ref, g_ref, w_ref, o_ref, *rest, emit_xn):
    xn_ref = rest[-1]

    @pl.when(pl.program_id(1) == 0)
    def _():
        y = _rms(x_ref[...], g_ref[...]).astype(BF16)
        xn_ref[...] = y
        if emit_xn:
            rest[0][...] = y

    o_ref[...] = _dot(xn_ref[...], w_ref[...]).astype(o_ref.dtype)


def _norm_mm(x, g, w, out_dtype, tm, tn, emit_xn=False):
    t, d = x.shape
    n = w.shape[1]
    out_shape = [jax.ShapeDtypeStruct((t, n), out_dtype)]
    out_specs = [pl.BlockSpec((tm, tn), lambda i, j: (i, j))]
    if emit_xn:
        out_shape.append(jax.ShapeDtypeStruct((t, d), BF16))
        out_specs.append(pl.BlockSpec((tm, d), lambda i, j: (i, 0)))
    res = pl.pallas_call(
        functools.partial(_norm_mm_body, emit_xn=emit_xn),
        grid=(t // tm, n // tn),
        in_specs=[pl.BlockSpec((tm, d), lambda i, j: (i, 0)),
                  pl.BlockSpec((1, d), lambda i, j: (0, 0)),
                  pl.BlockSpec((d, tn), lambda i, j: (0, j))],
        out_specs=out_specs,
        out_shape=out_shape,
        scratch_shapes=[pltpu.VMEM((tm, d), BF16)],
        compiler_params=_cparams(("parallel", "arbitrary")),
        name="norm_mm",
    )(x, g, w)
    return res if emit_xn else res[0]


def _causal_mask(i, tq):
    row = i * tq + lax.broadcasted_iota(jnp.int32, (tq, (i + 1) * tq), 0)
    col = lax.broadcasted_iota(jnp.int32, (tq, (i + 1) * tq), 1)
    return col <= row


def _softmax_terms(s, mask):
    s = jnp.where(mask, s, NEG)
    e = jnp.exp(s - jnp.max(s, axis=-1, keepdims=True))
    return e, 1.0 / jnp.sum(e, axis=-1, keepdims=True)


def _per_query_block(nq, fn):
    for i in range(nq):
        pl.when(pl.program_id(2) == i)(functools.partial(fn, i))


def _diff_attn_body(lam_ref, q_ref, k_ref, v_ref, g_ref, o_ref, *, tq, nq, lam_init):
    def block(i):
        kv = (i + 1) * tq
        mask = _causal_mask(i, tq)
        q = q_ref[...] * (DIFF_QK ** -0.5)
        lane = lax.broadcasted_iota(jnp.int32, q.shape, 1)
        k = k_ref[0:kv, :].astype(BF16)
        e1, r1 = _softmax_terms(_dot_nt(jnp.where(lane < DIFF_QK, q, 0.0).astype(BF16), k), mask)
        e2, r2 = _softmax_terms(_dot_nt(jnp.where(lane >= DIFF_QK, q, 0.0).astype(BF16), k), mask)
        a = e1 * r1 - e2 * (lam_ref[0, 0] * r2)
        o = _dot(a.astype(BF16), v_ref[0:kv, :].astype(BF16))
        o_ref[...] = _rms(o, g_ref[...]) * (1.0 - lam_init)

    _per_query_block(nq, block)


def _diff_attn(z, lam, g, batch, seq, lam_init, tq=256):
    nq = seq // tq
    return pl.pallas_call(
        functools.partial(_diff_attn_body, tq=tq, nq=nq, lam_init=lam_init),
        grid=(batch, HEADS, nq),
        in_specs=[pl.BlockSpec(memory_space=pltpu.SMEM),
                  pl.BlockSpec((tq, LANES), lambda b, h, i: (b * nq + i, C_DQ // LANES + h)),
                  pl.BlockSpec((seq, LANES), lambda b, h, i: (b, C_DK // LANES + h)),
                  pl.BlockSpec((seq, LANES), lambda b, h, i: (b, C_DV // LANES + h)),
                  pl.BlockSpec((1, LANES), lambda b, h, i: (0, h))],
        out_specs=pl.BlockSpec((tq, LANES), lambda b, h, i: (b * nq + i, h)),
        out_shape=jax.ShapeDtypeStruct((batch * seq, MIX_W), F32),
        compiler_params=_cparams(("parallel", "parallel", "arbitrary")),
        name="diff_attn",
    )(lam, z, z, z, g)


def _mla_proj_body(cq_ref, ckv_ref, kr_ref, krr_ref, pos_ref, inv_ref, qn_ref, kvn_ref,
                   wq_ref, wkv_ref, q_ref, k_ref, v_ref):
    qa = _dot(_rms(cq_ref[...], qn_ref[...]).astype(BF16), wq_ref[...])
    kva = _dot(_rms(ckv_ref[...], kvn_ref[...]).astype(BF16), wkv_ref[...])
    ang = pos_ref[...] * inv_ref[...]
    cos, sin = jnp.cos(ang), jnp.sin(ang)
    k_rope = (kr_ref[...] * cos + krr_ref[...] * sin).astype(BF16)
    scale = (MLA_NOPE + MLA_ROPE) ** -0.5
    for h in range(HEADS):
        lo, hi = h * LANES, (h + 1) * LANES
        q_rope = qa[:, MIX_W + lo:MIX_W + hi] * cos + qa[:, 2 * MIX_W + lo:2 * MIX_W + hi] * sin
        q_ref[:, 2 * lo:2 * lo + LANES] = (qa[:, lo:hi] * scale).astype(BF16)
        q_ref[:, 2 * lo + LANES:2 * hi] = (q_rope * scale).astype(BF16)
        k_ref[:, 2 * lo:2 * lo + LANES] = kva[:, lo:hi].astype(BF16)
        k_ref[:, 2 * lo + LANES:2 * hi] = k_rope
    v_ref[...] = kva[:, MIX_W:].astype(BF16)


def _mla_proj(z, pos, inv, qn, kvn, wq, wkv, tm=512):
    t = z.shape[0]
    row = lambda c, w: pl.BlockSpec((tm, w), lambda i: (i, c // w))
    full = lambda a: pl.BlockSpec(a.shape, lambda i: (0, 0))
    return pl.pallas_call(
        _mla_proj_body,
        grid=(t // tm,),
        in_specs=[row(C_CQ, MLA_Q_RANK), row(C_CKV, MLA_KV_RANK), row(C_KR, LANES), row(C_KRR, LANES),
                  pl.BlockSpec((tm, 1), lambda i: (i, 0)), full(inv), full(qn), full(kvn), full(wq), full(wkv)],
        out_specs=[pl.BlockSpec((tm, HEADS * MLA_QK_PAD), lambda i: (i, 0)),
                   pl.BlockSpec((tm, HEADS * MLA_QK_PAD), lambda i: (i, 0)),
                   pl.BlockSpec((tm, MIX_W), lambda i: (i, 0))],
        out_shape=[jax.ShapeDtypeStruct((t, HEADS * MLA_QK_PAD), BF16),
                   jax.ShapeDtypeStruct((t, HEADS * MLA_QK_PAD), BF16),
                   jax.ShapeDtypeStruct((t, MIX_W), BF16)],
        compiler_params=_cparams(("parallel",)),
        name="mla_proj",
    )(z, z, z, z, pos, inv, qn, kvn, wq, wkv)


def _mla_attn_body(q_ref, k_ref, v_ref, o_ref, *, tq, nq):
    def block(i):
        kv = (i + 1) * tq
        e, r = _softmax_terms(_dot_nt(q_ref[...], k_ref[0:kv, :]), _causal_mask(i, tq))
        o_ref[...] = _dot((e * r).astype(BF16), v_ref[0:kv, :])

    _per_query_block(nq, block)


def _mla_attn(q, k, v, batch, seq, tq=256):
    nq = seq // tq
    return pl.pallas_call(
        functools.partial(_mla_attn_body, tq=tq, nq=nq),
        grid=(batch, HEADS, nq),
        in_specs=[pl.BlockSpec((tq, MLA_QK_PAD), lambda b, h, i: (b * nq + i, h)),
                  pl.BlockSpec((seq, MLA_QK_PAD), lambda b, h, i: (b, h)),
                  pl.BlockSpec((seq, LANES), lambda b, h, i: (b, h))],
        out_specs=pl.BlockSpec((tq, LANES), lambda b, h, i: (b * nq + i, h)),
        out_shape=jax.ShapeDtypeStruct((batch * seq, MIX_W), F32),
        compiler_params=_cparams(("parallel", "parallel", "arbitrary")),
        name="mla_attn",
    )(q, k, v)


def _hgrn_body(q_ref, f_ref, v_ref, g_ref, par_ref, o_ref, st_ref):
    @pl.when(pl.program_id(1) == 0)
    def _():
        st_ref[...] = jnp.zeros_like(st_ref)

    sub_row = lax.broadcasted_iota(jnp.int32, (SUB, SUB), 0)
    sub_col = lax.broadcasted_iota(jnp.int32, (SUB, SUB), 1)
    for h in range(HEADS):
        sl = slice(h * LANES, (h + 1) * LANES)
        log_lb, log_1m_lb, one_m_lb, gain = (par_ref[r:r + 1, sl] for r in range(4))
        zf = f_ref[:, sl]
        q = q_ref[:, sl] * (HGRN_DK ** -0.5)
        v = v_ref[:, sl].astype(BF16)
        k = one_m_lb * jax.nn.sigmoid(-zf)
        lf_a, lf_b = log_lb, log_1m_lb + _log_sigmoid(zf)
        log_f = jnp.maximum(lf_a, lf_b) + jnp.log1p(jnp.exp(-jnp.abs(lf_a - lf_b)))
        b = _cumsum_rows(log_f)
        st = st_ref[h]
        o = _dot_nt((q * jnp.exp(b)).astype(BF16), st.astype(BF16))
        parts = []
        for i in range(HGRN_CHUNK // SUB):
            lo = i * SUB
            qi, bi, ki, vi = q[lo:lo + SUB], b[lo:lo + SUB], k[lo:lo + SUB], v[lo:lo + SUB]
            sc = jnp.zeros((SUB, SUB), F32)
            for s in range(SUB):
                dec = jnp.exp(jnp.minimum(bi - bi[s:s + 1], 0.0))
                col = jnp.sum(qi * ki[s:s + 1] * dec, axis=-1, keepdims=True)
                sc = sc + jnp.where(sub_col == s, col, 0.0)
            sc = jnp.where(sub_col <= sub_row, sc, 0.0)
            part = _dot(sc.astype(BF16), vi)
            if i > 0:
                b0 = b[lo - 1:lo]
                qe = (qi * jnp.exp(bi - b0)).astype(BF16)
                ke = (k[:lo] * jnp.exp(b0 - b[:lo])).astype(BF16)
                part = part + _dot(_dot_nt(qe, ke).astype(BF16), v[:lo])
            parts.append(part)
        o = o + jnp.concatenate(parts, axis=0)
        b_last = b[HGRN_CHUNK - 1:HGRN_CHUNK]
        st_ref[h] = st * jnp.exp(b_last) + _dot_tn(v, (k * jnp.exp(b_last - b)).astype(BF16))
        o_ref[:, sl] = _rms(o, gain) * jax.nn.silu(g_ref[:, sl])


def _hgrn(z, par, batch, seq):
    nc = seq // HGRN_CHUNK
    blk = lambda c: pl.BlockSpec((HGRN_CHUNK, MIX_W), lambda b, i: (b * nc + i, c // MIX_W))
    return pl.pallas_call(
        _hgrn_body,
        grid=(batch, nc),
        in_specs=[blk(C_GQ), blk(C_GF), blk(C_GV), blk(C_GG), pl.BlockSpec((8, MIX_W), lambda b, i: (0, 0))],
        out_specs=pl.BlockSpec((HGRN_CHUNK, MIX_W), lambda b, i: (b * nc + i, 0)),
        out_shape=jax.ShapeDtypeStruct((batch * seq, MIX_W), F32),
        scratch_shapes=[pltpu.VMEM((HEADS, HEAD_V, HGRN_DK), F32)],
        compiler_params=_cparams(("parallel", "arbitrary")),
        name="hgrn2",
    )(z, z, z, z, par)


def _conv_silu_body(x_ref, w_ref, b_ref, o_ref):
    x = x_ref[...]
    row = lax.broadcasted_iota(jnp.int32, x.shape, 0)
    y = x * w_ref[MLSTM_CONV - 1:MLSTM_CONV] + b_ref[...]
    for k in range(1, MLSTM_CONV):
        y = y + jnp.where(row >= k, pltpu.roll(x, k, 0), 0.0) * w_ref[MLSTM_CONV - 1 - k:MLSTM_CONV - k]
    o_ref[...] = jax.nn.silu(y)


def _conv_silu(z, w, b, batch, seq):
    wd = HEADS * MLSTM_DQK
    return pl.pallas_call(
        _conv_silu_body,
        grid=(batch, 2),
        in_specs=[pl.BlockSpec((seq, wd), lambda bi, j: (bi, C_LQ // wd + j)),
                  pl.BlockSpec((MLSTM_CONV, wd), lambda bi, j: (0, j)),
                  pl.BlockSpec((1, wd), lambda bi, j: (0, j))],
        out_specs=pl.BlockSpec((seq, wd), lambda bi, j: (bi, j)),
        out_shape=jax.ShapeDtypeStruct((batch * seq, 2 * wd), F32),
        compiler_params=_cparams(("parallel", "parallel")),
        name="conv_silu",
    )(z, w, b)


def _mlstm_body(qk_ref, v_ref, gate_ref, og_ref, gb_ref, norm_ref, o_ref, c_ref, m_ref):
    @pl.when(pl.program_id(1) == 0)
    def _():
        c_ref[...] = jnp.zeros_like(c_ref)
        m_ref[...] = jnp.zeros_like(m_ref)

    g = gate_ref[...] + gb_ref[...]
    lane = lax.broadcasted_iota(jnp.int32, g.shape, 1)
    gt = jnp.where(lane < HEADS, g, _cumsum_rows(_log_sigmoid(g)))
    gt_t = gt.T
    row = lax.broadcasted_iota(jnp.int32, (CHUNK, CHUNK), 0)
    col = lax.broadcasted_iota(jnp.int32, (CHUNK, CHUNK), 1)
    ones_col = jnp.where(lane == 0, 1.0, 0.0).astype(BF16)
    half_row = lax.broadcasted_iota(jnp.int32, (LANES, 1), 0) < MLSTM_DQK
    for p in range(HEADS // 2):
        q_t = qk_ref[:, p * LANES:(p + 1) * LANES] * (MLSTM_DQK ** -0.5)
        k_t = qk_ref[:, HEADS * MLSTM_DQK + p * LANES:HEADS * MLSTM_DQK + (p + 1) * LANES]
        c_old = c_ref[p]
        c_bf = c_old.astype(BF16)
        upd = jnp.zeros_like(c_old)
        carries = []
        for u in range(2):
            h = 2 * p + u
            mine = (lane >= u * MLSTM_DQK) & (lane < (u + 1) * MLSTM_DQK)
            li_col, b_col = gt[:, h:h + 1], gt[:, HEADS + h:HEADS + h + 1]
            li_row, b_row = gt_t[h:h + 1, :], gt_t[HEADS + h:HEADS + h + 1, :]
            m_st = m_ref[h:h + 1, 0:1]
            dmat = jnp.where(col <= row, b_col - b_row + li_row, NEG)
            inter_log = b_col + m_st
            m_t = jnp.maximum(inter_log, jnp.max(dmat, axis=-1, keepdims=True))
            qm = jnp.where(mine, q_t, 0.0).astype(BF16)
            s = _dot_nt(qm, k_t.astype(BF16)) * jnp.exp(dmat - m_t)
            v_aug = jnp.concatenate([v_ref[:, h * LANES:(h + 1) * LANES].astype(BF16), ones_col], axis=1)
            out = jnp.exp(inter_log - m_t) * _dot(qm, c_bf) + _dot(s.astype(BF16), v_aug)
            hid = out[:, :HEAD_V] / jnp.maximum(jnp.abs(out[:, HEAD_V:HEAD_V + 1]), jnp.exp(-m_t))
            sl = slice(h * LANES, (h + 1) * LANES)
            o_ref[:, sl] = _rms(hid, norm_ref[:, sl]) * jax.nn.sigmoid(og_ref[:, sl])
            b_last = b_col[CHUNK - 1:CHUNK]
            log_s = b_last - b_col + li_col
            m_new = jnp.maximum(b_last + m_st, jnp.max(log_s, axis=0, keepdims=True))
            carries.append(jnp.exp(b_last + m_st - m_new))
            kw = (jnp.where(mine, k_t, 0.0) * jnp.exp(log_s - m_new)).astype(BF16)
            upd = upd + _dot_tn(kw, v_aug)
            m_ref[h:h + 1, :] = jnp.broadcast_to(m_new, (1, LANES))
        c_ref[p] = jnp.where(half_row, carries[0], carries[1]) * c_old + upd


def _mlstm(qk, z, gb, norm, batch, seq):
    nc = seq // CHUNK
    return pl.pallas_call(
        _mlstm_body,
        grid=(batch, nc),
        in_specs=[pl.BlockSpec((CHUNK, 2 * HEADS * MLSTM_DQK), lambda b, i: (b * nc + i, 0)),
                  pl.BlockSpec((CHUNK, MIX_W), lambda b, i: (b * nc + i, C_LV // MIX_W)),
                  pl.BlockSpec((CHUNK, LANES), lambda b, i: (b * nc + i, C_GATE // LANES)),
                  pl.BlockSpec((CHUNK, MIX_W), lambda b, i: (b * nc + i, C_LO // MIX_W)),
                  pl.BlockSpec((1, LANES), lambda b, i: (0, 0)),
                  pl.BlockSpec((1, MIX_W), lambda b, i: (0, 0))],
        out_specs=pl.BlockSpec((CHUNK, MIX_W), lambda b, i: (b * nc + i, 0)),
        out_shape=jax.ShapeDtypeStruct((batch * seq, MIX_W), F32),
        scratch_shapes=[pltpu.VMEM((HEADS // 2, LANES, 2 * HEAD_V), F32), pltpu.VMEM((8, LANES), F32)],
        compiler_params=_cparams(("parallel", "arbitrary")),
        name="mlstm",
    )(qk, z, z, z, gb, norm)


def _out_proj_body(h_ref, y0_ref, y1_ref, y2_ref, y3_ref, g_ref, w_ref, o_ref):
    acc = h_ref[...]
    for i, y_ref in enumerate((y0_ref, y1_ref, y2_ref, y3_ref)):
        sl = slice(i * MIX_W, (i + 1) * MIX_W)
        acc = acc + _dot((y_ref[...] * g_ref[:, sl]).astype(BF16), w_ref[sl, :])
    o_ref[...] = acc


def _out_proj(h, ys, g, w, tm=512):
    t, d = h.shape
    yspec = pl.BlockSpec((tm, MIX_W), lambda i: (i, 0))
    return pl.pallas_call(
        _out_proj_body,
        grid=(t // tm,),
        in_specs=[pl.BlockSpec((tm, d), lambda i: (i, 0)), yspec, yspec, yspec, yspec,
                  pl.BlockSpec((1, HEADS * MIX_W), lambda i: (0, 0)),
                  pl.BlockSpec((HEADS * MIX_W, d), lambda i: (0, 0))],
        out_specs=pl.BlockSpec((tm, d), lambda i: (i, 0)),
        out_shape=jax.ShapeDtypeStruct((t, d), F32),
        compiler_params=_cparams(("parallel",)),
        name="out_proj",
    )(h, *ys, g, w)


def _ranked_top(x, n, dst_ref):
    rank = jnp.full(x.shape, float(n), F32)
    for r in range(n):
        mx = jnp.max(x, axis=0, keepdims=True)
        dst_ref[r:r + 1, :] = mx
        eq = x == mx
        rank = jnp.where(eq, float(r), rank)
        x = jnp.where(eq, NEG, x)
    return rank


def _kth_and_next(x, k):
    cum = jnp.zeros((1, x.shape[1]), F32)
    kth = jnp.full((1, x.shape[1]), NEG, F32)
    nxt = kth
    for _ in range(k + 1):
        mx = jnp.max(x, axis=0, keepdims=True)
        eq = x == mx
        new = cum + jnp.sum(eq.astype(F32), axis=0, keepdims=True)
        kth = jnp.where((cum < k) & (new >= k), mx, kth)
        nxt = jnp.where((cum < k + 1) & (new >= k + 1), mx, nxt)
        cum = new
        x = jnp.where(eq, NEG, x)
    return kth, nxt


PAIR_COLS = tuple((PEER_TOPK + 1) // (i + 1) for i in range(PEER_TOPK + 1))
PAIR_OFFS = tuple(sum(PAIR_COLS[:i]) for i in range(PEER_TOPK + 2))
PAIR_ROWS = -(-PAIR_OFFS[-1] // SUBLANES) * SUBLANES


def _pack_halves(x):
    bits = pltpu.bitcast(x.astype(BF16).astype(F32), jnp.uint32)
    half = x.shape[0] // 2
    return (bits[half:] & jnp.uint32(0xFFFF0000)) | (bits[:half] >> 16)


def _peer_router_body(q_ref, sk_ref, nb_ref, e1_ref, rb_ref, e2_ref, a_ref, b_ref, g_ref):
    tt = q_ref.shape[0]
    for h in range(PEER_HEADS):
        s1 = _dot_nt(sk_ref[2 * h], q_ref[:, 2 * h * LANES:(2 * h + 1) * LANES])
        s2 = _dot_nt(sk_ref[2 * h + 1], q_ref[:, (2 * h + 1) * LANES:(2 * h + 2) * LANES])
        rank_a = _ranked_top(s1, PEER_TOPK + 1, a_ref)
        rank_b = _ranked_top(s2, PEER_TOPK + 1, b_ref)
        g_ref[PAIR_ROWS - SUBLANES:PAIR_ROWS, :] = jnp.full((SUBLANES, tt), NEG, F32)
        for i, n in enumerate(PAIR_COLS):
            g_ref[PAIR_OFFS[i]:PAIR_OFFS[i] + n, :] = a_ref[i:i + 1, :] + b_ref[0:n, :]
        pair_sums = g_ref[...]
        kth, nxt = _kth_and_next(pair_sums, PEER_TOPK)
        thr = 0.5 * (kth + nxt)
        a_top, b_top = a_ref[0:1, :], b_ref[0:1, :]
        chosen = pair_sums >= thr
        zsum = jnp.sum(jnp.where(chosen, jnp.exp(pair_sums - (a_top + b_top)), 0.0), axis=0, keepdims=True)
        ones = jnp.where(chosen, 1.0, 0.0)
        count_a = jnp.zeros_like(s1)
        for i in range(PEER_TOPK):
            n_i = jnp.sum(ones[PAIR_OFFS[i]:PAIR_OFFS[i + 1]], axis=0, keepdims=True)
            count_a = jnp.where(rank_a == float(i), n_i, count_a)
        e1 = jnp.exp(s1 - a_top).astype(BF16).astype(F32)
        for q in range(tt // LANES):
            nb_ref[q, h] = count_a[:, q * LANES:(q + 1) * LANES]
            e1_ref[q, h] = e1[:, q * LANES:(q + 1) * LANES]
        rb_ref[h] = _pack_halves(rank_b)
        e2_ref[h] = _pack_halves(jnp.exp(s2 - b_top) * (0.5 / zsum))


def _peer_router(q, sk, tt=256):
    t = q.shape[0]
    row_spec = pl.BlockSpec((tt // LANES, PEER_HEADS, PEER_KEYS, LANES), lambda i: (i, 0, 0, 0))
    row_shape = jax.ShapeDtypeStruct((t // LANES, PEER_HEADS, PEER_KEYS, LANES), F32)
    key_spec = pl.BlockSpec((PEER_HEADS, PEER_KEYS // 2, tt), lambda i: (0, 0, i))
    key_shape = jax.ShapeDtypeStruct((PEER_HEADS, PEER_KEYS // 2, t), jnp.uint32)
    return pl.pallas_call(
        _peer_router_body,
        grid=(t // tt,),
        in_specs=[pl.BlockSpec((tt, q.shape[1]), lambda i: (i, 0)),
                  pl.BlockSpec(sk.shape, lambda i: (0, 0, 0))],
        out_specs=[row_spec, row_spec, key_spec, key_spec],
        out_shape=[row_shape, row_shape, key_shape, key_shape],
        scratch_shapes=[pltpu.VMEM((24, tt), F32), pltpu.VMEM((24, tt), F32),
                        pltpu.VMEM((PAIR_ROWS, tt), F32)],
        compiler_params=_cparams(("parallel",)),
        name="peer_router",
    )(q, sk)


PEER_MROWS = 256


def _peer_dense_body(xt_ref, u_ref, vt_ref, nb_ref, e1_ref, rb_ref, e2_ref, o_ref, act_ref, wa_ref, *, rows):
    @pl.when(pl.program_id(1) == 0)
    def _():
        o_ref[...] = jnp.zeros_like(o_ref)

    et, tt = act_ref.shape
    half = PEER_KEYS // 2
    for m in range(et // PEER_MROWS):
        ms = slice(m * PEER_MROWS, (m + 1) * PEER_MROWS)
        act_ref[ms, :] = _dot(u_ref[ms, :], xt_ref[...])

    def gelu2(z):
        return z + z * lax.erf(z * (2.0 ** -0.5))

    for a in range(rows):
        for q in range(tt // LANES):
            tok = slice(q * LANES, (q + 1) * LANES)

            def row(ref, h):
                bits = pltpu.bitcast(jnp.broadcast_to(ref[q, h, a:a + 1, :], (SUBLANES, LANES)), jnp.uint32)
                return pltpu.bitcast((bits & jnp.uint32(0xFFFF0000)) | (bits >> 16), BF16)

            nbs = [row(nb_ref, h) for h in range(PEER_HEADS)]
            e1s = [row(e1_ref, h) for h in range(PEER_HEADS)]
            for s in range(half // SUBLANES):
                ks = slice(s * SUBLANES, (s + 1) * SUBLANES)
                w = None
                for h in range(PEER_HEADS):
                    rb = pltpu.bitcast(rb_ref[h, ks, tok], BF16)
                    e2 = pltpu.bitcast(e2_ref[h, ks, tok], BF16)
                    t = jnp.where(rb < nbs[h], e2 * e1s[h], 0.0)
                    w = t if w is None else w + t
                wbits = pltpu.bitcast(w, jnp.uint32)
                lo = slice(a * PEER_KEYS + s * SUBLANES, a * PEER_KEYS + (s + 1) * SUBLANES)
                hi = slice(a * PEER_KEYS + half + s * SUBLANES, a * PEER_KEYS + half + (s + 1) * SUBLANES)
                wa_ref[lo, tok] = (pltpu.bitcast(wbits << 16, F32) * gelu2(act_ref[lo, tok])).astype(BF16)
                wa_ref[hi, tok] = (pltpu.bitcast(wbits & jnp.uint32(0xFFFF0000), F32)
                                   * gelu2(act_ref[hi, tok])).astype(BF16)
    for m in range(o_ref.shape[0] // PEER_MROWS):
        ms = slice(m * PEER_MROWS, (m + 1) * PEER_MROWS)
        o_ref[ms, :] += _dot(vt_ref[ms, :], wa_ref[...])


def _peer_dense(xt, u, vt, nb, e1, rb, e2, tt=512, rows=8):
    d, t = xt.shape
    et = rows * PEER_KEYS
    tok_blk = pl.BlockSpec((d, tt), lambda i, j: (0, i))
    row_blk = pl.BlockSpec((tt // LANES, PEER_HEADS, rows, LANES), lambda i, j: (i, 0, j, 0))
    key_blk = pl.BlockSpec((PEER_HEADS, PEER_KEYS // 2, tt), lambda i, j: (0, 0, i))
    return pl.pallas_call(
        functools.partial(_peer_dense_body, rows=rows),
        grid=(t // tt, PEER_EXPERTS // et),
        in_specs=[tok_blk, pl.BlockSpec((et, d), lambda i, j: (j, 0)), pl.BlockSpec((d, et), lambda i, j: (0, j)),
                  row_blk, row_blk, key_blk, key_blk],
        out_specs=tok_blk,
        out_shape=jax.ShapeDtypeStruct((d, t), F32),
        scratch_shapes=[pltpu.VMEM((et, tt), F32), pltpu.VMEM((et, tt), BF16)],
        compiler_params=_cparams(("parallel", "arbitrary")),
        name="peer_dense",
    )(xt, u, vt, nb, e1, rb, e2)


def _ple_body(h_ref, yt_ref, g_ref, wg_ref, p_ref, wp_ref, fg_ref, o_ref, *, final):
    h = h_ref[...] + yt_ref[...].T
    gate = jax.nn.sigmoid(_dot(_rms(h, g_ref[...]).astype(BF16), wg_ref[...]))
    out = h + gate * _dot(p_ref[...].astype(BF16), wp_ref[...])
    o_ref[...] = _rms(out, fg_ref[...]) if final else out


def _ple(h, yt, g, wg, p, wp, fg, final, tm=512):
    t, d = h.shape
    full = lambda a: pl.BlockSpec(a.shape, lambda i: (0, 0))
    return pl.pallas_call(
        functools.partial(_ple_body, final=final),
        grid=(t // tm,),
        in_specs=[pl.BlockSpec((tm, d), lambda i: (i, 0)), pl.BlockSpec((d, tm), lambda i: (0, i)), full(g), full(wg),
                  pl.BlockSpec((tm, p.shape[1]), lambda i: (i, 0)), full(wp), full(fg)],
        out_specs=pl.BlockSpec((tm, d), lambda i: (i, 0)),
        out_shape=jax.ShapeDtypeStruct((t, d), F32),
        compiler_params=_cparams(("parallel",)),
        name="ple",
    )(h, yt, g, wg, p, wp, fg)


def _cast_body(x_ref, o_ref):
    o_ref[...] = x_ref[...].astype(o_ref.dtype)


def _cast_layer(x, layer, dtype, rows):
    _, r, c = x.shape
    return pl.pallas_call(
        _cast_body,
        grid=(r // rows,),
        in_specs=[pl.BlockSpec((None, rows, c), lambda i: (layer, i, 0))],
        out_specs=pl.BlockSpec((rows, c), lambda i: (i, 0)),
        out_shape=jax.ShapeDtypeStruct((r, c), dtype),
        compiler_params=_cparams(("parallel",)),
        name="cast_layer",
    )(x)


def _cast_transpose_body(x_ref, o_ref):
    o_ref[...] = x_ref[...].T.astype(o_ref.dtype)


def _cast_transpose(x, layer, dtype, tile=1024):
    _, r, c = x.shape
    return pl.pallas_call(
        _cast_transpose_body,
        grid=(r // tile, c // tile),
        in_specs=[pl.BlockSpec((None, tile, tile), lambda i, j: (layer, i, j))],
        out_specs=pl.BlockSpec((tile, tile), lambda i, j: (j, i)),
        out_shape=jax.ShapeDtypeStruct((c, r), dtype),
        compiler_params=_cparams(("parallel", "parallel")),
        name="cast_transpose",
    )(x)


def _rot_half_cols(w):
    half = w.shape[-1] // 2
    return jnp.concatenate([-w[..., half:], w[..., :half]], axis=-1)


def _pad_cols(w, width):
    return jnp.pad(w, [(0, 0)] * (w.ndim - 1) + [(0, width - w.shape[-1])])


def _prep_w_in(w_in):
    offs = [0]
    for s in IN_SIZES:
        offs.append(offs[-1] + s)
    (d_q, d_k, d_v, g_q, g_f, g_v, g_g, m_cq, m_ckv, m_kr,
     l_q, l_k, l_v, l_i, l_f, l_o) = (w_in[..., offs[i]:offs[i + 1]] for i in range(len(IN_SIZES)))
    pieces = [d_q, d_k, d_v, g_q, g_f, g_v, g_g, l_v, l_o, m_ckv, l_q, l_k, m_cq,
              _pad_cols(m_kr, LANES), _pad_cols(_rot_half_cols(m_kr), LANES),
              _pad_cols(jnp.concatenate([l_i, l_f], axis=-1), LANES)]
    return jnp.concatenate(pieces, axis=-1)


def _prep_mla(w_uq, w_ukv):
    nl = w_uq.shape[0]
    wq = w_uq.reshape(nl, MLA_Q_RANK, HEADS, MLA_NOPE + MLA_ROPE)
    rope = wq[..., MLA_NOPE:]
    wq_all = jnp.concatenate([
        wq[..., :MLA_NOPE].reshape(nl, MLA_Q_RANK, MIX_W),
        _pad_cols(rope, LANES).reshape(nl, MLA_Q_RANK, MIX_W),
        _pad_cols(_rot_half_cols(rope), LANES).reshape(nl, MLA_Q_RANK, MIX_W)], axis=-1)
    wkv = w_ukv.reshape(nl, MLA_KV_RANK, HEADS, MLA_NOPE + HEAD_V)
    wkv_all = jnp.concatenate([wkv[..., :MLA_NOPE].reshape(nl, MLA_KV_RANK, MIX_W),
                               wkv[..., MLA_NOPE:].reshape(nl, MLA_KV_RANK, MIX_W)], axis=-1)
    return wq_all.astype(BF16), wkv_all.astype(BF16)


def kernel(x, p, positions, ln_mix, w_in, diff_lambda, diff_norm, hgrn_lb_logits, hgrn_norm, mla_q_norm,
           mla_kv_norm, mla_w_uq, mla_w_ukv, mlstm_conv_w, mlstm_conv_b, mlstm_gate_b, mlstm_norm, group_gain,
           w_out, ln_ffn, peer_w_q, peer_subkeys, peer_u, peer_v, ln_ple, ple_w_gate, ple_w_proj, ln_final):
    batch, seq, d = x.shape
    depth = w_in.shape[0]
    t = batch * seq

    wq_all, wkv_all = _prep_mla(mla_w_uq, mla_w_ukv)
    subkeys_b = peer_subkeys.reshape(depth, PEER_HEADS * 2, PEER_KEYS, PEER_KEY_DIM).astype(BF16)

    lv = diff_lambda.astype(F32)
    lam = jnp.exp(jnp.sum(lv[:, 0] * lv[:, 1], axis=-1)) - jnp.exp(jnp.sum(lv[:, 2] * lv[:, 3], axis=-1))
    lbs = jnp.cumsum(jax.nn.softmax(hgrn_lb_logits.astype(F32), axis=0), axis=0)
    lbs = lbs - lbs[:1]
    hgrn_par = jnp.stack([jnp.maximum(jnp.log(lbs), NEG), jnp.log1p(-lbs), 1.0 - lbs, hgrn_norm], axis=1)
    hgrn_par = jnp.pad(hgrn_par, ((0, 0), (0, 4), (0, 0)))
    gate_b = _pad_cols(mlstm_gate_b.reshape(depth, 1, 2 * HEADS), LANES)
    half = MLA_ROPE // 2
    inv = ROPE_THETA ** (-jnp.arange(half, dtype=F32) / half)
    inv = _pad_cols(jnp.concatenate([inv, inv])[None, :], LANES)
    pos = positions.astype(F32).reshape(t, 1)
    row = lambda a: a.reshape(1, -1)

    h = x.reshape(t, d)
    for i in range(depth):
        lam_init = 0.8 - 0.6 * math.exp(-0.3 * i)
        cast = lambda a, rows: _cast_layer(a, i, BF16, rows)
        z = _norm_mm(h, row(ln_mix[i]), _prep_w_in(w_in[i].astype(BF16)), F32, tm=512, tn=2048)
        y_diff = _diff_attn(z, (lam[i] + lam_init).reshape(1, 1), row(diff_norm[i]), batch, seq, lam_init)
        y_hgrn = _hgrn(z, hgrn_par[i], batch, seq)
        mq, mk, mv = _mla_proj(z, pos, inv, row(mla_q_norm[i]), row(mla_kv_norm[i]), wq_all[i], wkv_all[i])
        y_mla = _mla_attn(mq, mk, mv, batch, seq)
        qk = _conv_silu(z, mlstm_conv_w[i], row(mlstm_conv_b[i]), batch, seq)
        y_mlstm = _mlstm(qk, z, gate_b[i], row(mlstm_norm[i]), batch, seq)
        h = _out_proj(h, (y_diff, y_hgrn, y_mla, y_mlstm), row(group_gain[i]), cast(w_out, 1024))
        pq, xn = _norm_mm(h, row(ln_ffn[i]), cast(peer_w_q, 1024), BF16, tm=512, tn=2048, emit_xn=True)
        yt = _peer_dense(xn.T, cast(peer_u, 2048), _cast_transpose(peer_v, i, BF16),
                         *_peer_router(pq, subkeys_b[i]))
        h = _ple(h, yt, row(ln_ple[i]), cast(ple_w_gate, 1024), p[i].reshape(t, PLE_DIM),
                 ple_w_proj[i].astype(BF16), row(ln_final), final=(i == depth - 1))
    return h.reshape(batch, seq, d)
```
